```python
import jax
import jax.numpy as jnp
from jax import lax
import numpy as np

D_MODEL = 2048
BATCH = 8
SEQ = 4096
DEPTH = 2

CHUNK = 64
N_MIXERS = 2
N_LAYERS_A = (DEPTH + 1) // 2
N_LAYERS_B = DEPTH // 2
N_META = 16
NORM_EPS = 1e-6
RWKV_HEAD_SIZE = 64
RWKV_HEADS = D_MODEL // RWKV_HEAD_SIZE
DECAY_LORA = 96
ICLR_LORA = 96
GATE_LORA = 256
GN_EPS = 64e-5
KK_EPS = 1e-24
CONV_WIDTH = 31
LN_EPS = 1e-5
PEER_HEADS = 8
N_KEYS = 128
N_EXPERTS = N_KEYS * N_KEYS
PEER_DK = 256
PEER_TOPK = 16
PEER_BLOCK = 128

kernel_name = 'hybrid_rwkv7_conformer_peer_trunk'


def rms_norm(x, g):
    xf = x.astype(jnp.float32)
    y = xf * lax.rsqrt(jnp.mean(xf * xf, axis=-1, keepdims=True) + NORM_EPS)
    return (y * g.astype(jnp.float32)).astype(x.dtype)


def token_shift(x):
    return jnp.pad(x, ((0, 0), (1, 0), (0, 0)))[:, :-1]


def rwkv7_time_mix(x, mix, w0, w1, w2, a0, a1, a2, g1, g2, k_k, k_a, r_k, w_rkv, w_o, lnx_w, lnx_b):
    bsz, t, d = x.shape
    f32 = jnp.float32
    xx = token_shift(x) - x
    xr = x + xx * mix[0]
    xw = x + xx * mix[1]
    xk = x + xx * mix[2]
    xv = x + xx * mix[3]
    xa = x + xx * mix[4]
    xg = x + xx * mix[5]
    rkv = jnp.einsum('nbtd,nde->nbte', jnp.stack([xr, xk, xv]), w_rkv)
    r, k, v = rkv[0], rkv[1], rkv[2]
    w_log = -jax.nn.softplus(-(w0 + jnp.tanh(xw @ w1) @ w2)) - 0.5
    decay = jnp.exp(-jnp.exp(w_log.astype(f32)))
    a = jax.nn.sigmoid(a0 + (xa @ a1) @ a2)
    g = jax.nn.sigmoid(xg @ g1) @ g2

    def heads(z):
        return z.reshape(bsz, t, RWKV_HEADS, RWKV_HEAD_SIZE)

    kk = heads(k * k_k).astype(f32)
    kk = kk * lax.rsqrt(jnp.maximum(jnp.sum(kk * kk, axis=-1, keepdims=True), KK_EPS))
    k = k * (1.0 + (a - 1.0) * k_a)
    rh = heads(r).astype(f32)
    kh = heads(k).astype(f32)
    vh = heads(v).astype(f32)
    ah = heads(a).astype(f32)

    def tm(z):
        return jnp.moveaxis(z, 1, 0)

    def step(state, inp):
        r_t, w_t, k_t, v_t, kk_t, b_t = inp
        sa = jnp.einsum('bhvk,bhk->bhv', state, kk_t)
        state = (state * w_t[:, :, None, :]
                 - sa[..., None] * b_t[:, :, None, :]
                 + v_t[..., None] * k_t[:, :, None, :])
        return state, jnp.einsum('bhvk,bhk->bhv', state, r_t)

    s0 = jnp.zeros((bsz, RWKV_HEADS, RWKV_HEAD_SIZE, RWKV_HEAD_SIZE), f32)
    _, y = lax.scan(step, s0, (tm(rh), tm(heads(decay)), tm(kh), tm(vh), tm(kk), tm(kk * ah)))
    y = jnp.moveaxis(y, 0, 1)
    mu = jnp.mean(y, axis=-1, keepdims=True)
    var = jnp.mean(jnp.square(y - mu), axis=-1, keepdims=True)
    yn = ((y - mu) * lax.rsqrt(var + GN_EPS)).reshape(bsz, t, d)
    yn = yn * lnx_w.astype(f32) + lnx_b.astype(f32)
    bonus = jnp.sum(rh * kh * r_k.astype(f32), axis=-1, keepdims=True) * vh
    out = (yn + bonus.reshape(bsz, t, d)) * g.astype(f32)
    return out.astype(x.dtype) @ w_o


def conformer_conv(x, pw1_w, pw1_b, dw_w, dw_b, ln_w, ln_b, pw2_w, pw2_b):
    d = x.shape[-1]
    h = x @ pw1_w + pw1_b
    h = h[..., :d] * jax.nn.sigmoid(h[..., d:])
    h = lax.conv_general_dilated(
        h, dw_w[:, None, :], window_strides=(1,),
        padding=((CONV_WIDTH - 1, 0),),
        dimension_numbers=('NWC', 'WIO', 'NWC'),
        feature_group_count=d) + dw_b
    hf = h.astype(jnp.float32)
    mu = jnp.mean(hf, axis=-1, keepdims=True)
    var = jnp.mean(jnp.square(hf - mu), axis=-1, keepdims=True)
    hn = (hf - mu) * lax.rsqrt(var + LN_EPS) * ln_w.astype(jnp.float32) + ln_b.astype(jnp.float32)
    hn = jax.nn.silu(hn).astype(x.dtype)
    return hn @ pw2_w + pw2_b


def peer_ffn(x, wq, keys, u_tab, v_tab):
    bsz, t, d = x.shape
    n_tok = bsz * t
    n_blk = -(-n_tok // PEER_BLOCK)
    tok = jnp.pad(x.reshape(n_tok, d), ((0, n_blk * PEER_BLOCK - n_tok), (0, 0)))
    tok = tok.reshape(n_blk, PEER_BLOCK, d)

    def block(xb):
        q = (xb @ wq).reshape(PEER_BLOCK, PEER_HEADS, 2, PEER_DK // 2)
        s = jnp.einsum('thcd,hcnd->thcn', q, keys)
        sv, si = lax.top_k(s, PEER_TOPK)
        cand = sv[:, :, 0, :, None] + sv[:, :, 1, None, :]
        cidx = si[:, :, 0, :, None] * N_KEYS + si[:, :, 1, None, :]
        cand = cand.reshape(PEER_BLOCK, PEER_HEADS, PEER_TOPK * PEER_TOPK)
        cidx = cidx.reshape(PEER_BLOCK, PEER_HEADS, PEER_TOPK * PEER_TOPK)
        top_s, pos = lax.top_k(cand, PEER_TOPK)
        eidx = jnp.take_along_axis(cidx, pos, axis=-1)
        gate = jax.nn.softmax(top_s.astype(jnp.float32), axis=-1)
        u = u_tab[eidx]
        act = jax.nn.gelu(jnp.einsum('td,thkd->thk', xb, u).astype(jnp.float32), approximate=False)
        v = v_tab[eidx]
        return jnp.einsum('thk,thkd->td', (gate * act).astype(xb.dtype), v)

    out = lax.map(block, tok).reshape(n_blk * PEER_BLOCK, d)[:n_tok]
    return out.reshape(bsz, t, d)


def setup_inputs(seed: int = 0) -> dict:
    key = jax.random.key(seed)
    ks = jax.random.split(key, 40)
    f32 = jnp.float32
    D = D_MODEL
    NA, NB = N_LAYERS_A, N_LAYERS_B

    def nrm(k, shape, scale):
        return jax.random.normal(k, shape, f32) * scale

    return {
        'x': nrm(ks[0], (BATCH, SEQ, D), 1.0),
        'meta_tokens': nrm(ks[1], (N_META, D), 1.0),
        'norm_mix_a': 1.0 + nrm(ks[2], (NA, D), 0.02),
        'rwkv_mix': jax.random.uniform(ks[3], (NA, 6, D), f32),
        'rwkv_w0': jax.random.uniform(ks[4], (NA, D), f32, -5.0, -0.5),
        'rwkv_w1': nrm(ks[5], (NA, D, DECAY_LORA), D ** -0.5),
        'rwkv_w2': nrm(ks[6], (NA, DECAY_LORA, D), 0.1 * DECAY_LORA ** -0.5),
        'rwkv_a0': nrm(ks[7], (NA, D), 0.5),
        'rwkv_a1': nrm(ks[8], (NA, D, ICLR_LORA), D ** -0.5),
        'rwkv_a2': nrm(ks[9], (NA, ICLR_LORA, D), ICLR_LORA ** -0.5),
        'rwkv_g1': nrm(ks[10], (NA, D, GATE_LORA), D ** -0.5),
        'rwkv_g2': nrm(ks[11], (NA, GATE_LORA, D), GATE_LORA ** -0.5),
        'rwkv_k_k': 0.85 + nrm(ks[12], (NA, D), 0.05),
        'rwkv_k_a': 1.0 + nrm(ks[13], (NA, D), 0.05),
        'rwkv_r_k': nrm(ks[14], (NA, RWKV_HEADS, RWKV_HEAD_SIZE), 0.1),
        'rwkv_w_rkv': nrm(ks[15], (NA, 3, D, D), D ** -0.5),
        'rwkv_w_o': nrm(ks[16], (NA, D, D), D ** -0.5),
        'rwkv_lnx_w': 1.0 + nrm(ks[17], (NA, D), 0.02),
        'rwkv_lnx_b': nrm(ks[18], (NA, D), 0.02),
        'norm_mix_b': 1.0 + nrm(ks[19], (NB, D), 0.02),
        'conv_pw1_w': nrm(ks[20], (NB, D, 2 * D), D ** -0.5),
        'conv_pw1_b': nrm(ks[21], (NB, 2 * D), 0.02),
        'conv_dw_w': nrm(ks[22], (NB, CONV_WIDTH, D), CONV_WIDTH ** -0.5),
        'conv_dw_b': nrm(ks[23], (NB, D), 0.02),
        'conv_ln_w': 1.0 + nrm(ks[24], (NB, D), 0.02),
        'conv_ln_b': nrm(ks[25], (NB, D), 0.02),
        'conv_pw2_w': nrm(ks[26], (NB, D, D), D ** -0.5),
        'conv_pw2_b': nrm(ks[27], (NB, D), 0.02),
        'norm_ffn': 1.0 + nrm(ks[28], (DEPTH, D), 0.02),
        'peer_wq': nrm(ks[29], (DEPTH, D, PEER_HEADS * PEER_DK), D ** -0.5),
        'peer_keys': nrm(ks[30], (DEPTH, PEER_HEADS, 2, N_KEYS, PEER_DK // 2), (PEER_DK // 2) ** -0.5),
        'peer_u': nrm(ks[31], (DEPTH, N_EXPERTS, D), D ** -0.5),
        'peer_v': nrm(ks[32], (DEPTH, N_EXPERTS, D), PEER_HEADS ** -0.5),
        'norm_final': 1.0 + nrm(ks[33], (D,), 0.02),
    }


def reference(x, meta_tokens, norm_mix_a, rwkv_mix, rwkv_w0, rwkv_w1, rwkv_w2, rwkv_a0, rwkv_a1,
              rwkv_a2, rwkv_g1, rwkv_g2, rwkv_k_k, rwkv_k_a, rwkv_r_k, rwkv_w_rkv, rwkv_w_o,
              rwkv_lnx_w, rwkv_lnx_b, norm_mix_b, conv_pw1_w, conv_pw1_b, conv_dw_w, conv_dw_b,
              conv_ln_w, conv_ln_b, conv_pw2_w, conv_pw2_b, norm_ffn, peer_wq, peer_keys,
              peer_u, peer_v, norm_final):
    bsz = x.shape[0]
    meta = jnp.broadcast_to(meta_tokens[None].astype(x.dtype), (bsz, N_META, D_MODEL))
    h = jnp.concatenate([meta, x], axis=1)
    for i in range(DEPTH):
        j = i // N_MIXERS
        if i % N_MIXERS == 0:
            h = h + rwkv7_time_mix(
                rms_norm(h, norm_mix_a[j]), rwkv_mix[j], rwkv_w0[j], rwkv_w1[j], rwkv_w2[j],
                rwkv_a0[j], rwkv_a1[j], rwkv_a2[j], rwkv_g1[j], rwkv_g2[j], rwkv_k_k[j],
                rwkv_k_a[j], rwkv_r_k[j], rwkv_w_rkv[j], rwkv_w_o[j], rwkv_lnx_w[j], rwkv_lnx_b[j])
        else:
            h = h + conformer_conv(
                rms_norm(h, norm_mix_b[j]), conv_pw1_w[j], conv_pw1_b[j], conv_dw_w[j],
                conv_dw_b[j], conv_ln_w[j], conv_ln_b[j], conv_pw2_w[j], conv_pw2_b[j])
        h = h + peer_ffn(rms_norm(h, norm_ffn[i]), peer_wq[i], peer_keys[i], peer_u[i], peer_v[i])
    return rms_norm(h, norm_final)[:, N_META:]
```

```python
import functools
import math

import jax
import jax.numpy as jnp
from jax import lax
from jax.experimental import pallas as pl
from jax.experimental.pallas import tpu as pltpu

F32 = jnp.float32
BF16 = jnp.bfloat16

NORM_EPS = 1e-6
GN_EPS = 64e-5
KK_EPS = 1e-24
LN_EPS = 1e-5
HEAD = 64
CHUNK = 64
SUB = 16
PEER_TOPK = 16
N_META_PAD_TO = CHUNK
VMEM_LIMIT = 56 * 1024 * 1024


def _divisor(n, target, mult):
    best = None
    for d in range(mult, min(n, target) + 1, mult):
        if n % d == 0:
            best = d
    assert best is not None, (n, target, mult)
    return best


def _params(*sem):
    return pltpu.CompilerParams(dimension_semantics=sem, vmem_limit_bytes=VMEM_LIMIT)


def _mm_kernel(a_ref, w_ref, b_ref, o_ref):
    acc = jnp.dot(a_ref[...], w_ref[...], preferred_element_type=F32)
    o_ref[...] = (acc + b_ref[...]).astype(o_ref.dtype)


def _matmul(a, w, bias=None, out_dtype=F32):
    m, k = a.shape
    n = w.shape[1]
    tm = _divisor(m, 1024, 16)
    tn = _divisor(n, 1024, 128)
    if bias is None:
        bias = jnp.zeros((n,), F32)
    return pl.pallas_call(
        _mm_kernel,
        grid=(n // tn, m // tm),
        in_specs=[
            pl.BlockSpec((tm, k), lambda j, i: (i, 0)),
            pl.BlockSpec((k, tn), lambda j, i: (0, j)),
            pl.BlockSpec((1, tn), lambda j, i: (0, j)),
        ],
        out_specs=pl.BlockSpec((tm, tn), lambda j, i: (i, j)),
        out_shape=jax.ShapeDtypeStruct((m, n), out_dtype),
        compiler_params=_params("parallel", "parallel"),
        name="matmul",
    )(a, w, bias.reshape(1, n).astype(F32))


def _mm_t_kernel(w_ref, a_ref, o_ref):
    o_ref[...] = lax.dot_general(
        w_ref[...], a_ref[...], (((1,), (1,)), ((), ())),
        preferred_element_type=F32).astype(o_ref.dtype)


def _matmul_t(wt, a, out_dtype=F32):
    n, k = wt.shape
    m = a.shape[0]
    tm = _divisor(m, 1024, 128)
    tn = _divisor(n, 1024, 128)
    return pl.pallas_call(
        _mm_t_kernel,
        grid=(n // tn, m // tm),
        in_specs=[
            pl.BlockSpec((tn, k), lambda j, i: (j, 0)),
            pl.BlockSpec((tm, k), lambda j, i: (i, 0)),
        ],
        out_specs=pl.BlockSpec((tn, tm), lambda j, i: (j, i)),
        out_shape=jax.ShapeDtypeStruct((n, m), out_dtype),
        compiler_params=_params("parallel", "parallel"),
        name="matmul_t",
    )(wt, a)


def _lora_kernel(a_ref, w1_ref, w2_ref, b_ref, o_ref, *, mid):
    h = jnp.dot(a_ref[...], w1_ref[...], preferred_element_type=F32)
    if mid == "tanh":
        h = jnp.tanh(h)
    elif mid == "sigmoid":
        h = jax.nn.sigmoid(h)
    o = jnp.dot(h.astype(BF16), w2_ref[...], preferred_element_type=F32)
    o_ref[...] = o + b_ref[...]


def _lora(a, w1, w2, bias, mid):
    m, k = a.shape
    r = w1.shape[1]
    n = w2.shape[1]
    rp = -(-r // 128) * 128
    w1p = jnp.pad(w1, ((0, 0), (0, rp - r))).astype(BF16)
    w2p = jnp.pad(w2, ((0, rp - r), (0, 0))).astype(BF16)
    tm = _divisor(m, 1024, 16)
    if bias is None:
        bias = jnp.zeros((n,), F32)
    return pl.pallas_call(
        functools.partial(_lora_kernel, mid=mid),
        grid=(m // tm,),
        in_specs=[
            pl.BlockSpec((tm, k), lambda i: (i, 0)),
            pl.BlockSpec((k, rp), lambda i: (0, 0)),
            pl.BlockSpec((rp, n), lambda i: (0, 0)),
            pl.BlockSpec((1, n), lambda i: (0, 0)),
        ],
        out_specs=pl.BlockSpec((tm, n), lambda i: (i, 0)),
        out_shape=jax.ShapeDtypeStruct((m, n), F32),
        compiler_params=_params("parallel"),
        name="lora",
    )(a, w1p, w2p, bias.reshape(1, n).astype(F32))


def _bdot(a, b):
    return jnp.dot(a.astype(BF16), b.astype(BF16), preferred_element_type=F32)


def _bdot_nt(a, b):
    return lax.dot_general(a.astype(BF16), b.astype(BF16), (((1,), (1,)), ((), ())),
                           preferred_element_type=F32)


def _bdot_tn(a, b):
    return lax.dot_general(a.astype(BF16), b.astype(BF16), (((0,), (0,)), ((), ())),
                           preferred_element_type=F32)


def _unit_lower_inverse(a, eye, same_blk):
    d = jnp.where(same_blk, a, 0.0)
    l = a - d
    p = eye - d
    dk = d
    for _ in range(int(math.log2(SUB)) - 1):
        dk = _bdot(dk, dk)
        p = p + _bdot(p, dk)
    e = _bdot(p, l)
    q = eye - e
    ek = e
    for _ in range(int(math.log2(CHUNK // SUB)) - 1):
        ek = _bdot(ek, ek)
        q = q + _bdot(q, ek)
    return _bdot(q, p)


def _scan_kernel(r_ref, lw_ref, k_ref, v_ref, kk_ref, b_ref, y_ref, s_ref, *, nheads):
    c = CHUNK

    @pl.when(pl.program_id(2) == 0)
    def _():
        s_ref[...] = jnp.zeros_like(s_ref)

    row = lax.broadcasted_iota(jnp.int32, (c, c), 0)
    col = lax.broadcasted_iota(jnp.int32, (c, c), 1)
    strict = row > col
    incl = row >= col
    same_blk = (row // SUB) == (col // SUB)
    eye = jnp.where(row == col, 1.0, 0.0).astype(F32)
    ones_tril = jnp.where(incl, 1.0, 0.0).astype(F32)

    lw = lw_ref[0]
    g = jnp.dot(ones_tril, lw, preferred_element_type=F32, precision=lax.Precision.HIGHEST)
    gc = g[c - 1:c, :]
    eg = jnp.exp(g)
    eng = jnp.exp(-g)
    egc = jnp.exp(gc - g)
    kk_t = kk_ref[0] * jnp.exp(g - lw)
    r_t = r_ref[0] * eg
    k_h = k_ref[0] * eng
    b_h = b_ref[0] * eng
    k_g = k_ref[0] * egc
    b_g = b_ref[0] * egc
    dec = jnp.exp(gc)
    v_all = v_ref[0]

    for hd in range(nheads):
        sl = slice(hd * HEAD, (hd + 1) * HEAD)
        m1 = jnp.concatenate([kk_t[:, sl], r_t[:, sl]], axis=0)
        m2 = jnp.concatenate([b_h[:, sl], k_h[:, sl]], axis=0)
        gm = _bdot_nt(m1, m2)
        a_b = jnp.where(strict, gm[:c, :c], 0.0)
        a_k = jnp.where(strict, gm[:c, c:], 0.0)
        a_rb = jnp.where(incl, gm[c:, :c], 0.0)
        a_rk = jnp.where(incl, gm[c:, c:], 0.0)
        t_inv = _unit_lower_inverse(a_b, eye, same_blk)
        s0 = s_ref[hd]
        v = v_all[:, sl]
        x1 = _bdot_nt(m1, s0)
        x2 = _bdot(jnp.concatenate([a_k, a_rk], axis=0), v)
        u = _bdot(t_inv, x1[:c] + x2[:c])
        y = x1[c:] + x2[c:] - _bdot(a_rb, u)
        y_ref[0, :, sl] = y
        vu = jnp.concatenate([v, u], axis=0)
        kb = jnp.concatenate([k_g[:, sl], -b_g[:, sl]], axis=0)
        s_ref[hd] = s0 * dec[:, sl] + _bdot_tn(vu, kb)


def _rwkv_scan(r, lw, k, v, kk, b):
    bsz, tp, d = r.shape
    lanes = _divisor(d, 256, 128)
    nheads = lanes // HEAD
    spec = pl.BlockSpec((1, CHUNK, lanes), lambda i, j, c: (i, c, j))
    return pl.pallas_call(
        functools.partial(_scan_kernel, nheads=nheads),
        grid=(bsz, d // lanes, tp // CHUNK),
        in_specs=[spec] * 6,
        out_specs=spec,
        out_shape=jax.ShapeDtypeStruct((bsz, tp, d), F32),
        scratch_shapes=[pltpu.VMEM((nheads, HEAD, HEAD), F32)],
        compiler_params=_params("parallel", "parallel", "arbitrary"),
        name="rwkv_scan",
    )(r, lw, k, v, kk, b)


HALO = 32


def _conv_kernel(prev_ref, cur_ref, w_ref, b_ref, o_ref, buf_ref, *, width, tt):
    first = pl.program_id(1) == 0
    buf_ref[0:HALO, :] = jnp.where(first, 0.0, prev_ref[0])
    buf_ref[HALO:HALO + tt, :] = cur_ref[0]
    acc = jnp.zeros(o_ref.shape[1:], F32) + b_ref[...]
    off = HALO - (width - 1)
    for j in range(width):
        acc = acc + buf_ref[off + j:off + j + tt, :] * w_ref[j:j + 1, :]
    o_ref[0] = acc


def _dwconv(x, w, bias):
    bsz, tp, d = x.shape
    width = w.shape[0]
    assert width - 1 <= HALO
    tt = _divisor(tp, 512, HALO)
    td = _divisor(d, 512, 128)
    ratio = tt // HALO
    wpad = jnp.pad(w, ((0, HALO - width), (0, 0)))
    return pl.pallas_call(
        functools.partial(_conv_kernel, width=width, tt=tt),
        grid=(bsz, tp // tt, d // td),
        in_specs=[
            pl.BlockSpec((1, HALO, td), lambda i, t, j: (i, jnp.maximum(t * ratio - 1, 0), j)),
            pl.BlockSpec((1, tt, td), lambda i, t, j: (i, t, j)),
            pl.BlockSpec((HALO, td), lambda i, t, j: (0, j)),
            pl.BlockSpec((1, td), lambda i, t, j: (0, j)),
        ],
        out_specs=pl.BlockSpec((1, tt, td), lambda i, t, j: (i, t, j)),
        out_shape=jax.ShapeDtypeStruct((bsz, tp, d), F32),
        scratch_shapes=[pltpu.VMEM((HALO + tt, td), F32)],
        compiler_params=_params("parallel", "parallel", "parallel"),
        name="dwconv",
    )(x, x, wpad, bias.reshape(1, d))


NEG = -jnp.inf
N_CAND = 16 + 7 * 8 + 8


def _cand_static():
    row = lax.broadcasted_iota(jnp.int32, (N_CAND, 1), 0)
    grp = jnp.where(row < 16, 0, (row - 16) // 8 + 1)
    bmid = (row - 16) % 8
    last = row >= 72
    a = jnp.where(row < 16, 0, jnp.where(last, row - 64, grp))
    b = jnp.where(row < 16, row, jnp.where(last, 0, bmid))
    valid = (a + 1) * (b + 1) <= PEER_TOPK
    return a * PEER_TOPK + b, valid


def _retrieve_kernel(q_ref, keys_ref, g_ref, i_ref, j_ref,
                     sv_ref, si_ref, cs_ref, ci_ref, cj_ref, ts_ref, *, nheads, nkeys, tt):
    kiota = lax.broadcasted_iota(jnp.int32, (nkeys, tt), 0)
    flat, valid = _cand_static()

    def head_body(h, carry):
        for half in range(2):
            hc = h * 2 + half
            q = q_ref[pl.ds(pl.multiple_of(hc * nkeys, nkeys), nkeys), :]
            s = jnp.dot(keys_ref[hc], q.astype(BF16), preferred_element_type=F32)
            for r in range(PEER_TOPK):
                m = jnp.max(s, axis=0, keepdims=True)
                idx = jnp.min(jnp.where(s == m, kiota, nkeys), axis=0, keepdims=True)
                sv_ref[half, r:r + 1, :] = m
                si_ref[half, r:r + 1, :] = idx
                s = jnp.where(kiota == idx, NEG, s)
        cs_ref[0:16, :] = sv_ref[0, 0:1, :] + sv_ref[1, :, :]
        ci_ref[0:16, :] = jnp.broadcast_to(si_ref[0, 0:1, :], (16, tt))
        cj_ref[0:16, :] = si_ref[1, :, :]
        for a in range(1, 8):
            lo = 16 + (a - 1) * 8
            cs_ref[lo:lo + 8, :] = sv_ref[0, a:a + 1, :] + sv_ref[1, 0:8, :]
            ci_ref[lo:lo + 8, :] = jnp.broadcast_to(si_ref[0, a:a + 1, :], (8, tt))
            cj_ref[lo:lo + 8, :] = si_ref[1, 0:8, :]
        cs_ref[72:80, :] = sv_ref[0, 8:16, :] + sv_ref[1, 0:1, :]
        ci_ref[72:80, :] = si_ref[0, 8:16, :]
        cj_ref[72:80, :] = jnp.broadcast_to(si_ref[1, 0:1, :], (8, tt))
        cand = jnp.where(valid, cs_ref[...], NEG)
        ci = ci_ref[...]
        cj = cj_ref[...]
        base = pl.multiple_of(h * PEER_TOPK, PEER_TOPK)
        for r in range(PEER_TOPK):
            m = jnp.max(cand, axis=0, keepdims=True)
            f = jnp.min(jnp.where(cand == m, flat, 1 << 20), axis=0, keepdims=True)
            sel = flat == f
            ts_ref[r:r + 1, :] = m
            i_ref[pl.ds(base + r, 1), :] = jnp.max(jnp.where(sel, ci, -1), axis=0, keepdims=True)
            j_ref[pl.ds(base + r, 1), :] = jnp.max(jnp.where(sel, cj, -1), axis=0, keepdims=True)
            cand = jnp.where(sel, NEG, cand)
        top = ts_ref[...]
        ex = jnp.exp(top - jnp.max(top, axis=0, keepdims=True))
        g_ref[pl.ds(base, PEER_TOPK), :] = ex / jnp.sum(ex, axis=0, keepdims=True)
        return carry

    lax.fori_loop(0, nheads, head_body, 0)


def _peer_retrieve(qt, keys):
    ph, _, nkeys, dk = keys.shape
    m = qt.shape[1]
    assert nkeys == dk == 128
    tt = _divisor(m, 256, 128)
    keys2 = keys.reshape(ph * 2, nkeys, dk).astype(BF16)
    np_ = ph * PEER_TOPK
    out_spec = pl.BlockSpec((np_, tt), lambda i: (0, i))
    return pl.pallas_call(
        functools.partial(_retrieve_kernel, nheads=ph, nkeys=nkeys, tt=tt),
        grid=(m // tt,),
        in_specs=[
            pl.BlockSpec((ph * 2 * dk, tt), lambda i: (0, i)),
            pl.BlockSpec((ph * 2, nkeys, dk), lambda i: (0, 0, 0)),
        ],
        out_specs=[out_spec, out_spec, out_spec],
        out_shape=[
            jax.ShapeDtypeStruct((np_, m), F32),
            jax.ShapeDtypeStruct((np_, m), jnp.int32),
            jax.ShapeDtypeStruct((np_, m), jnp.int32),
        ],
        scratch_shapes=[
            pltpu.VMEM((2, PEER_TOPK, tt), F32),
            pltpu.VMEM((2, PEER_TOPK, tt), jnp.int32),
            pltpu.VMEM((N_CAND, tt), F32),
            pltpu.VMEM((N_CAND, tt), jnp.int32),
            pltpu.VMEM((N_CAND, tt), jnp.int32),
            pltpu.VMEM((PEER_TOPK, tt), F32),
        ],
        compiler_params=_params("parallel"),
        name="peer_retrieve",
    )(qt, keys2)


def _gate_table_kernel(g_ref, i_ref, j_ref, o_ref, *, tb, nkeys):
    sub = lax.broadcasted_iota(jnp.int32, (nkeys, g_ref.shape[1]), 0)

    def body(t, carry):
        g = g_ref[pl.ds(t, 1), :]
        i = i_ref[pl.ds(t, 1), :]
        j = j_ref[pl.ds(t, 1), :]
        a_t = jnp.where(i == sub, g, 0.0).astype(BF16)
        b_t = jnp.where(j == sub, 1.0, 0.0).astype(BF16)
        w = lax.dot_general(a_t, b_t, (((1,), (1,)), ((), ())), preferred_element_type=F32)
        o_ref[t] = w.astype(o_ref.dtype)
        return carry

    lax.fori_loop(0, tb, body, 0)


def _gate_table(g, i, j, nkeys):
    m, p = g.shape
    tb = _divisor(m, 128, 8)
    spec = pl.BlockSpec((tb, p), lambda t: (t, 0))
    return pl.pallas_call(
        functools.partial(_gate_table_kernel, tb=tb, nkeys=nkeys),
        grid=(m // tb,),
        in_specs=[spec, spec, spec],
        out_specs=pl.BlockSpec((tb, nkeys, nkeys), lambda t: (t, 0, 0)),
        out_shape=jax.ShapeDtypeStruct((m, nkeys, nkeys), BF16),
        compiler_params=_params("parallel"),
        name="peer_gate_table",
    )(g, i, j)


def _peer_dense_kernel(x_ref, ut_ref, v_ref, w_ref, o_ref, acc_ref):
    k = pl.program_id(1)

    @pl.when(k == 0)
    def _():
        acc_ref[...] = jnp.zeros_like(acc_ref)

    act = jnp.dot(x_ref[...], ut_ref[...], preferred_element_type=F32)
    gelu = 0.5 * act * (1.0 + lax.erf(act * (1.0 / math.sqrt(2.0))))
    p = (w_ref[...].astype(F32) * gelu).astype(BF16)
    acc_ref[...] += jnp.dot(p, v_ref[...], preferred_element_type=F32)

    @pl.when(k == pl.num_programs(1) - 1)
    def _():
        o_ref[...] = acc_ref[...]


def _peer_dense(x, ut, v, w):
    m, d = x.shape
    e = ut.shape[1]
    tm = _divisor(m, 1280, 16)
    te = _divisor(e, 512, 128)
    return pl.pallas_call(
        _peer_dense_kernel,
        grid=(m // tm, e // te),
        in_specs=[
            pl.BlockSpec((tm, d), lambda i, k: (i, 0)),
            pl.BlockSpec((d, te), lambda i, k: (0, k)),
            pl.BlockSpec((te, d), lambda i, k: (k, 0)),
            pl.BlockSpec((tm, te), lambda i, k: (i, k)),
        ],
        out_specs=pl.BlockSpec((tm, d), lambda i, k: (i, 0)),
        out_shape=jax.ShapeDtypeStruct((m, d), F32),
        scratch_shapes=[pltpu.VMEM((tm, d), F32)],
        compiler_params=_params("parallel", "arbitrary"),
        name="peer_dense",
    )(x, ut, v, w)


def _rms_norm(x, g):
    return x * lax.rsqrt(jnp.mean(x * x, axis=-1, keepdims=True) + NORM_EPS) * g


def _peer_layer(h, g_norm, wq, keys, u_tab, v_tab):
    bsz, tp, d = h.shape
    m = bsz * tp
    nkeys = keys.shape[2]
    xn = _rms_norm(h, g_norm).reshape(m, d).astype(BF16)
    qt = _matmul_t(wq.T.astype(BF16), xn)
    gate, ii, jj = _peer_retrieve(qt, keys)
    w = _gate_table(gate.T, ii.T, jj.T, nkeys).reshape(m, nkeys * nkeys)
    out = _peer_dense(xn, u_tab.T.astype(BF16), v_tab.astype(BF16), w)
    return h + out.reshape(bsz, tp, d)


def _rwkv_layer(h, g_norm, mix, w0, w1, w2, a0, a1, a2, g1, g2, k_k, k_a, r_k, w_rkv, w_o, lnx_w, lnx_b):
    bsz, tp, d = h.shape
    m = bsz * tp
    nh = d // HEAD
    x = _rms_norm(h, g_norm)
    xx = jnp.pad(x, ((0, 0), (1, 0), (0, 0)))[:, :-1] - x

    def mixed(n):
        return (x + xx * mix[n]).reshape(m, d).astype(BF16)

    r = _matmul(mixed(0), w_rkv[0].astype(BF16))
    k = _matmul(mixed(2), w_rkv[1].astype(BF16))
    v = _matmul(mixed(3), w_rkv[2].astype(BF16))
    w_log = -jax.nn.softplus(-_lora(mixed(1), w1, w2, w0, "tanh")) - 0.5
    lw = -jnp.exp(w_log)
    a = jax.nn.sigmoid(_lora(mixed(4), a1, a2, a0, None))
    g = _lora(mixed(5), g1, g2, None, "sigmoid")

    kk = (k * k_k).reshape(m, nh, HEAD)
    kk = (kk * lax.rsqrt(jnp.maximum(jnp.sum(kk * kk, axis=-1, keepdims=True), KK_EPS))).reshape(m, d)
    k = k * (1.0 + (a - 1.0) * k_a)
    b = kk * a

    def seq(z):
        return z.reshape(bsz, tp, d)

    y = _rwkv_scan(seq(r), seq(lw), seq(k), seq(v), seq(kk), seq(b)).reshape(m, nh, HEAD)
    mu = jnp.mean(y, axis=-1, keepdims=True)
    var = jnp.mean(jnp.square(y - mu), axis=-1, keepdims=True)
    yn = ((y - mu) * lax.rsqrt(var + GN_EPS)).reshape(m, d) * lnx_w + lnx_b
    rh = r.reshape(m, nh, HEAD)
    bonus = jnp.sum(rh * k.reshape(m, nh, HEAD) * r_k, axis=-1, keepdims=True) * v.reshape(m, nh, HEAD)
    out = (yn + bonus.reshape(m, d)) * g
    return h + _matmul(out.astype(BF16), w_o.astype(BF16)).reshape(bsz, tp, d)


def _conv_layer(h, g_norm, pw1_w, pw1_b, dw_w, dw_b, ln_w, ln_b, pw2_w, pw2_b):
    bsz, tp, d = h.shape
    m = bsz * tp
    x = _rms_norm(h, g_norm).reshape(m, d).astype(BF16)
    z = _matmul(x, pw1_w.astype(BF16), pw1_b)
    z = (z[:, :d] * jax.nn.sigmoid(z[:, d:])).reshape(bsz, tp, d)
    z = _dwconv(z, dw_w, dw_b)
    mu = jnp.mean(z, axis=-1, keepdims=True)
    var = jnp.mean(jnp.square(z - mu), axis=-1, keepdims=True)
    zn = (z - mu) * lax.rsqrt(var + LN_EPS) * ln_w + ln_b
    zn = jax.nn.silu(zn).reshape(m, d).astype(BF16)
    return h + _matmul(zn, pw2_w.astype(BF16), pw2_b).reshape(bsz, tp, d)


def kernel(x, meta_tokens, norm_mix_a, rwkv_mix, rwkv_w0, rwkv_w1, rwkv_w2, rwkv_a0, rwkv_a1, rwkv_a2, rwkv_g1, rwkv_g2, rwkv_k_k, rwkv_k_a, rwkv_r_k, rwkv_w_rkv, rwkv_w_o, rwkv_lnx_w, rwkv_lnx_b, norm_mix_b, conv_pw1_w, conv_pw1_b, conv_dw_w, conv_dw_b, conv_ln_w, conv_ln_b, conv_pw2_w, conv_pw2_b, norm_ffn, peer_wq, peer_keys, peer_u, peer_v, norm_final):
    bsz, seq, d = x.shape
    n_meta = meta_tokens.shape[0]
    t = n_meta + seq
    tp = -(-t // 128) * 128 if t <= 128 else -(-t // CHUNK) * CHUNK
    depth = norm_ffn.shape[0]
    meta = jnp.broadcast_to(meta_tokens[None].astype(x.dtype), (bsz, n_meta, d))
    h = jnp.concatenate([meta, x, jnp.zeros((bsz, tp - t, d), x.dtype)], axis=1)
    for i in range(depth):
        j = i // 2
        if i % 2 == 0:
            h = _rwkv_layer(
                h, norm_mix_a[j], rwkv_mix[j], rwkv_w0[j], rwkv_w1[j], rwkv_w2[j], rwkv_a0[j], rwkv_a1[j],
                rwkv_a2[j], rwkv_g1[j], rwkv_g2[j], rwkv_k_k[j], rwkv_k_a[j], rwkv_r_k[j], rwkv_w_rkv[j],
                rwkv_w_o[j], rwkv_lnx_w[j], rwkv_lnx_b[j])
        else:
            h = _conv_layer(
                h, norm_mix_b[j], conv_pw1_w[j], conv_pw1_b[j], conv_dw_w[j], conv_dw_b[j], conv_ln_w[j],
                conv_ln_b[j], conv_pw2_w[j], conv_pw2_b[j])
        h = _peer_layer(h, norm_ffn[i], peer_wq[i], peer_keys[i], peer_u[i], peer_v[i])
    return _rms_norm(h, norm_final)[:, n_meta:t]
```

```python
import functools
import math

import jax
import jax.numpy as jnp
from jax import lax
from jax.experimental import pallas as pl
from jax.experimental.pallas import tpu as pltpu

F32 = jnp.float32
BF16 = jnp.bfloat16

NORM_EPS = 1e-6
GN_EPS = 64e-5
KK_EPS = 1e-24
LN_EPS = 1e-5
HEAD = 64
CHUNK = 64
SUB = 16
PEER_TOPK = 16
N_META_PAD_TO = CHUNK
VMEM_LIMIT = 56 * 1024 * 1024


def _divisor(n, target, mult):
    best = None
    for d in range(mult, min(n, target) + 1, mult):
        if n % d == 0:
            best = d
    assert best is not None, (n, target, mult)
    return best


def _params(*sem):
    return pltpu.CompilerParams(dimension_semantics=sem, vmem_limit_bytes=VMEM_LIMIT)


def _mm_kernel(a_ref, w_ref, b_ref, o_ref):
    acc = jnp.dot(a_ref[...], w_ref[...], preferred_element_type=F32)
    o_ref[...] = (acc + b_ref[...]).astype(o_ref.dtype)


def _matmul(a, w, bias=None, out_dtype=F32):
    m, k = a.shape
    n = w.shape[1]
    tm = _divisor(m, 1024, 16)
    tn = _divisor(n, 1024, 128)
    if bias is None:
        bias = jnp.zeros((n,), F32)
    return pl.pallas_call(
        _mm_kernel,
        grid=(n // tn, m // tm),
        in_specs=[
            pl.BlockSpec((tm, k), lambda j, i: (i, 0)),
            pl.BlockSpec((k, tn), lambda j, i: (0, j)),
            pl.BlockSpec((1, tn), lambda j, i: (0, j)),
        ],
        out_specs=pl.BlockSpec((tm, tn), lambda j, i: (i, j)),
        out_shape=jax.ShapeDtypeStruct((m, n), out_dtype),
        compiler_params=_params("parallel", "parallel"),
        name="matmul",
    )(a, w, bias.reshape(1, n).astype(F32))


def _mm_t_kernel(w_ref, a_ref, o_ref):
    o_ref[...] = lax.dot_general(
        w_ref[...], a_ref[...], (((1,), (1,)), ((), ())),
        preferred_element_type=F32).astype(o_ref.dtype)


def _matmul_t(wt, a, out_dtype=F32):
    n, k = wt.shape
    m = a.shape[0]
    tm = _divisor(m, 1024, 128)
    tn = _divisor(n, 1024, 128)
    return pl.pallas_call(
        _mm_t_kernel,
        grid=(n // tn, m // tm),
        in_specs=[
            pl.BlockSpec((tn, k), lambda j, i: (j, 0)),
            pl.BlockSpec((tm, k), lambda j, i: (i, 0)),
        ],
        out_specs=pl.BlockSpec((tn, tm), lambda j, i: (j, i)),
        out_shape=jax.ShapeDtypeStruct((n, m), out_dtype),
        compiler_params=_params("parallel", "parallel"),
        name="matmul_t",
    )(wt, a)


def _lora_kernel(a_ref, w1_ref, w2_ref, b_ref, o_ref, *, mid):
    h = jnp.dot(a_ref[...], w1_ref[...], preferred_element_type=F32)
    if mid == "tanh":
        h = jnp.tanh(h)
    elif mid == "sigmoid":
        h = jax.nn.sigmoid(h)
    o = jnp.dot(h.astype(BF16), w2_ref[...], preferred_element_type=F32)
    o_ref[...] = o + b_ref[...]


def _lora(a, w1, w2, bias, mid):
    m, k = a.shape
    r = w1.shape[1]
    n = w2.shape[1]
    rp = -(-r // 128) * 128
    w1p = jnp.pad(w1, ((0, 0), (0, rp - r))).astype(BF16)
    w2p = jnp.pad(w2, ((0, rp - r), (0, 0))).astype(BF16)
    tm = _divisor(m, 1024, 16)
    if bias is None:
        bias = jnp.zeros((n,), F32)
    return pl.pallas_call(
        functools.partial(_lora_kernel, mid=mid),
        grid=(m // tm,),
        in_specs=[
            pl.BlockSpec((tm, k), lambda i: (i, 0)),
            pl.BlockSpec((k, rp), lambda i: (0, 0)),
            pl.BlockSpec((rp, n), lambda i: (0, 0)),
            pl.BlockSpec((1, n), lambda i: (0, 0)),
        ],
        out_specs=pl.BlockSpec((tm, n), lambda i: (i, 0)),
        out_shape=jax.ShapeDtypeStruct((m, n), F32),
        compiler_params=_params("parallel"),
        name="lora",
    )(a, w1p, w2p, bias.reshape(1, n).astype(F32))


def _bdot(a, b):
    return jnp.dot(a.astype(BF16), b.astype(BF16), preferred_element_type=F32)


def _bdot_nt(a, b):
    return lax.dot_general(a.astype(BF16), b.astype(BF16), (((1,), (1,)), ((), ())),
                           preferred_element_type=F32)


def _bdot_tn(a, b):
    return lax.dot_general(a.astype(BF16), b.astype(BF16), (((0,), (0,)), ((), ())),
                           preferred_element_type=F32)


def _big(x, lo_half):
    zero = jnp.zeros_like(x)
    return jnp.concatenate([jnp.where(lo_half, x, zero), jnp.where(lo_half, zero, x)], axis=0)


def _fold(x):
    c = x.shape[0] // 2
    return x[:c] + x[c:]


def _scan_prep_kernel(r_ref, lw_ref, k_ref, v_ref, kk_ref, b_ref,
                      p_ref, m_ref, y1_ref, n_ref, dec_ref, *, npairs):
    c = CHUNK
    c2 = 2 * c
    row = lax.broadcasted_iota(jnp.int32, (c2, c2), 0)
    col = lax.broadcasted_iota(jnp.int32, (c2, c2), 1)
    strict = (row % c) > (col % c)
    incl = (row % c) >= (col % c)
    same_blk = (row // SUB) == (col // SUB)
    eye = jnp.where(row == col, 1.0, 0.0).astype(F32)
    lo_half = lax.broadcasted_iota(jnp.int32, (c, c2), 1) < HEAD
    tri = lax.broadcasted_iota(jnp.int32, (c, c), 0) >= lax.broadcasted_iota(jnp.int32, (c, c), 1)
    ones_tril = jnp.where(tri, 1.0, 0.0).astype(F32)

    lw = lw_ref[0]
    g = jnp.dot(ones_tril, lw, preferred_element_type=F32, precision=lax.Precision.HIGHEST)
    gc = g[c - 1:c, :]
    eg = jnp.exp(g)
    eng = jnp.exp(-g)
    egc = jnp.exp(gc - g)
    kk_t = kk_ref[0] * jnp.exp(g - lw)
    r_t = r_ref[0] * eg
    k_h = k_ref[0] * eng
    b_h = b_ref[0] * eng
    k_g = k_ref[0] * egc
    b_g = b_ref[0] * egc
    v_all = v_ref[0]
    dec_ref[0, 0] = jnp.exp(gc)

    units = range(npairs)

    def pair(x, u):
        return _big(x[:, u * c2:(u + 1) * c2], lo_half)

    kkt = [pair(kk_t, u) for u in units]
    rt = [pair(r_t, u) for u in units]
    vb = [pair(v_all, u) for u in units]
    bg = [pair(b_g, u) for u in units]
    gm = [_bdot_nt(jnp.concatenate([kkt[u], rt[u]], axis=0),
                   jnp.concatenate([pair(b_h, u), pair(k_h, u)], axis=0)) for u in units]
    a_b = [jnp.where(strict, gm[u][:c2, :c2], 0.0) for u in units]
    a_k = [jnp.where(strict, gm[u][:c2, c2:], 0.0) for u in units]
    a_rb = [jnp.where(incl, gm[u][c2:, :c2], 0.0) for u in units]
    a_rk = [jnp.where(incl, gm[u][c2:, c2:], 0.0) for u in units]
    x2 = [_bdot(jnp.concatenate([a_k[u], a_rk[u]], axis=0), vb[u]) for u in units]

    d = [jnp.where(same_blk, a_b[u], 0.0) for u in units]
    low = [a_b[u] - d[u] for u in units]
    p = [eye - d[u] for u in units]
    dk = d
    for _ in range(int(math.log2(SUB)) - 1):
        dk = [_bdot(dk[u], dk[u]) for u in units]
        p = [p[u] + _bdot(p[u], dk[u]) for u in units]
    e = [_bdot(p[u], low[u]) for u in units]
    q = [eye - e[u] for u in units]
    ek = e
    for _ in range(int(math.log2(CHUNK // SUB)) - 1):
        ek = [_bdot(ek[u], ek[u]) for u in units]
        q = [q[u] + _bdot(q[u], ek[u]) for u in units]
    t_inv = [_bdot(q[u], p[u]) for u in units]

    wu = [_bdot(t_inv[u], jnp.concatenate([kkt[u], x2[u][:c2]], axis=1)) for u in units]
    aw = [_bdot(a_rb[u], wu[u]) for u in units]
    m0 = [_bdot_tn(-bg[u], wu[u][:, :c2]) for u in units]
    nt = [_bdot_tn(jnp.concatenate([vb[u], wu[u][:, c2:]], axis=0),
                   jnp.concatenate([pair(k_g, u), -bg[u]], axis=0)) for u in units]
    for u in units:
        sl = slice(u * c2, (u + 1) * c2)
        p_ref[0, :, sl] = _fold(rt[u] - aw[u][:, :c2]).astype(p_ref.dtype)
        y1_ref[0, :, sl] = _fold(x2[u][c2:] - aw[u][:, c2:])
        m_ref[0, :, sl] = _fold(m0[u]).astype(m_ref.dtype)
        n_ref[0, :, sl] = _fold(nt[u])


def _scan_seq_kernel(p_ref, m_ref, y1_ref, n_ref, dec_ref, y_ref, s_ref, *, npairs):
    c2 = 2 * CHUNK

    @pl.when(pl.program_id(2) == 0)
    def _():
        s_ref[...] = jnp.zeros_like(s_ref)

    lo_half = lax.broadcasted_iota(jnp.int32, (CHUNK, c2), 1) < HEAD
    for u in range(npairs):
        sl = slice(u * c2, (u + 1) * c2)
        s = s_ref[u]
        y_ref[0, :, sl] = _fold(_bdot_nt(_big(p_ref[0, :, sl], lo_half), s)) + y1_ref[0, :, sl]
        s_ref[u] = (s * dec_ref[0, 0, :, sl] + _bdot_nt(s, _big(m_ref[0, :, sl], lo_half))
                    + _big(n_ref[0, :, sl], lo_half))


def _rwkv_scan(r, lw, k, v, kk, b):
    bsz, tp, d = r.shape
    nc = tp // CHUNK
    lanes = _divisor(d, 1024, 128)
    spec = pl.BlockSpec((1, CHUNK, lanes), lambda i, j, c: (i, c, j))
    dspec = pl.BlockSpec((1, 1, 1, lanes), lambda i, j, c: (i, c, 0, j))
    tile =jax.ShapeDtypeStruct((bsz, tp, d), F32)
    tile16 = jax.ShapeDtypeStruct((bsz, tp, d), BF16)
    p, m0, y1, nt, dec = pl.pallas_call(
        functools.partial(_scan_prep_kernel, npairs=lanes // 128),
        grid=(bsz, d // lanes, nc),
        in_specs=[spec] * 6,
        out_specs=[spec, spec, spec, spec, dspec],
        out_shape=[tile16, tile16, tile, tile, jax.ShapeDtypeStruct((bsz, nc, 1, d), F32)],
        compiler_params=_params("parallel", "parallel", "parallel"),
        name="rwkv_scan_prep",
    )(r, lw, k, v, kk, b)
    lanes = _divisor(d, 1024, 128)
    spec = pl.BlockSpec((1, CHUNK, lanes), lambda i, j, c: (i, c, j))
    dspec = pl.BlockSpec((1, 1, 1, lanes), lambda i, j, c: (i, c, 0, j))
    return pl.pallas_call(
        functools.partial(_scan_seq_kernel, npairs=lanes // 128),
        grid=(bsz, d // lanes, nc),
        in_specs=[spec, spec, spec, spec, dspec],
        out_specs=spec,
        out_shape=tile,
        scratch_shapes=[pltpu.VMEM((lanes // 128, 2 * CHUNK, 2 * CHUNK), F32)],
        compiler_params=_params("parallel", "parallel", "arbitrary"),
        name="rwkv_scan_seq",
    )(p, m0, y1, nt, dec)


HALO = 32


def _conv_kernel(prev_ref, cur_ref, w_ref, b_ref, o_ref, buf_ref, *, width, tt):
    first = pl.program_id(1) == 0
    buf_ref[0:HALO, :] = jnp.where(first, 0.0, prev_ref[0])
    buf_ref[HALO:HALO + tt, :] = cur_ref[0]
    acc = jnp.zeros(o_ref.shape[1:], F32) + b_ref[...]
    off = HALO - (width - 1)
    for j in range(width):
        acc = acc + buf_ref[off + j:off + j + tt, :] * w_ref[j:j + 1, :]
    o_ref[0] = acc


def _dwconv(x, w, bias):
    bsz, tp, d = x.shape
    width = w.shape[0]
    assert width - 1 <= HALO
    tt = _divisor(tp, 512, HALO)
    td = _divisor(d, 512, 128)
    ratio = tt // HALO
    wpad = jnp.pad(w, ((0, HALO - width), (0, 0)))
    return pl.pallas_call(
        functools.partial(_conv_kernel, width=width, tt=tt),
        grid=(bsz, tp // tt, d // td),
        in_specs=[
            pl.BlockSpec((1, HALO, td), lambda i, t, j: (i, jnp.maximum(t * ratio - 1, 0), j)),
            pl.BlockSpec((1, tt, td), lambda i, t, j: (i, t, j)),
            pl.BlockSpec((HALO, td), lambda i, t, j: (0, j)),
            pl.BlockSpec((1, td), lambda i, t, j: (0, j)),
        ],
        out_specs=pl.BlockSpec((1, tt, td), lambda i, t, j: (i, t, j)),
        out_shape=jax.ShapeDtypeStruct((bsz, tp, d), F32),
        scratch_shapes=[pltpu.VMEM((HALO + tt, td), F32)],
        compiler_params=_params("parallel", "parallel", "parallel"),
        name="dwconv",
    )(x, x, wpad, bias.reshape(1, d))


NEG = -jnp.inf
N_CAND = 16 + 7 * 8 + 8


def _cand_static():
    row = lax.broadcasted_iota(jnp.int32, (N_CAND, 1), 0)
    grp = jnp.where(row < 16, 0, (row - 16) // 8 + 1)
    bmid = (row - 16) % 8
    last = row >= 72
    a = jnp.where(row < 16, 0, jnp.where(last, row - 64, grp))
    b = jnp.where(row < 16, row, jnp.where(last, 0, bmid))
    valid = (a + 1) * (b + 1) <= PEER_TOPK
    return a * PEER_TOPK + b, valid


def _retrieve_kernel(q_ref, keys_ref, g_ref, i_ref, j_ref,
                     sv_ref, si_ref, cs_ref, ci_ref, cj_ref, ts_ref, *, nheads, nkeys, tt):
    kiota = lax.broadcasted_iota(jnp.int32, (nkeys, tt), 0)
    flat, valid = _cand_static()

    def head_body(h, carry):
        for half in range(2):
            hc = h * 2 + half
            q = q_ref[pl.ds(pl.multiple_of(hc * nkeys, nkeys), nkeys), :]
            s = jnp.dot(keys_ref[hc], q.astype(BF16), preferred_element_type=F32)
            for r in range(PEER_TOPK):
                m = jnp.max(s, axis=0, keepdims=True)
                idx = jnp.min(jnp.where(s == m, kiota, nkeys), axis=0, keepdims=True)
                sv_ref[half, r:r + 1, :] = m
                si_ref[half, r:r + 1, :] = idx
                s = jnp.where(kiota == idx, NEG, s)
        cs_ref[0:16, :] = sv_ref[0, 0:1, :] + sv_ref[1, :, :]
        ci_ref[0:16, :] = jnp.broadcast_to(si_ref[0, 0:1, :], (16, tt))
        cj_ref[0:16, :] = si_ref[1, :, :]
        for a in range(1, 8):
            lo = 16 + (a - 1) * 8
            cs_ref[lo:lo + 8, :] = sv_ref[0, a:a + 1, :] + sv_ref[1, 0:8, :]
            ci_ref[lo:lo + 8, :] = jnp.broadcast_to(si_ref[0, a:a + 1, :], (8, tt))
            cj_ref[lo:lo + 8, :] = si_ref[1, 0:8, :]
        cs_ref[72:80, :] = sv_ref[0, 8:16, :] + sv_ref[1, 0:1, :]
        ci_ref[72:80, :] = si_ref[0, 8:16, :]
        cj_ref[72:80, :] = jnp.broadcast_to(si_ref[1, 0:1, :], (8, tt))
        cand = jnp.where(valid, cs_ref[...], NEG)
        ci = ci_ref[...]
        cj = cj_ref[...]
        base = pl.multiple_of(h * PEER_TOPK, PEER_TOPK)
        for r in range(PEER_TOPK):
            m = jnp.max(cand, axis=0, keepdims=True)
            f = jnp.min(jnp.where(cand == m, flat, 1 << 20), axis=0, keepdims=True)
            sel = flat == f
            ts_ref[r:r + 1, :] = m
            i_ref[pl.ds(base + r, 1), :] = jnp.max(jnp.where(sel, ci, -1), axis=0, keepdims=True)
            j_ref[pl.ds(base + r, 1), :] = jnp.max(jnp.where(sel, cj, -1), axis=0, keepdims=True)
            cand = jnp.where(sel, NEG, cand)
        top = ts_ref[...]
        ex = jnp.exp(top - jnp.max(top, axis=0, keepdims=True))
        g_ref[pl.ds(base, PEER_TOPK), :] = ex / jnp.sum(ex, axis=0, keepdims=True)
        return carry

    lax.fori_loop(0, nheads, head_body, 0)


def _peer_retrieve(qt, keys):
    ph, _, nkeys, dk = keys.shape
    m = qt.shape[1]
    assert nkeys == dk == 128
    tt = _divisor(m, 256, 128)
    keys2 = keys.reshape(ph * 2, nkeys, dk).astype(BF16)
    np_ = ph * PEER_TOPK
    out_spec = pl.BlockSpec((np_, tt), lambda i: (0, i))
    return pl.pallas_call(
        functools.partial(_retrieve_kernel, nheads=ph, nkeys=nkeys, tt=tt),
        grid=(m // tt,),
        in_specs=[
            pl.BlockSpec((ph * 2 * dk, tt), lambda i: (0, i)),
            pl.BlockSpec((ph * 2, nkeys, dk), lambda i: (0, 0, 0)),
        ],
        out_specs=[out_spec, out_spec, out_spec],
        out_shape=[
            jax.ShapeDtypeStruct((np_, m), F32),
            jax.ShapeDtypeStruct((np_, m), jnp.int32),
            jax.ShapeDtypeStruct((np_, m), jnp.int32),
        ],
        scratch_shapes=[
            pltpu.VMEM((2, PEER_TOPK, tt), F32),
            pltpu.VMEM((2, PEER_TOPK, tt), jnp.int32),
            pltpu.VMEM((N_CAND, tt), F32),
            pltpu.VMEM((N_CAND, tt), jnp.int32),
            pltpu.VMEM((N_CAND, tt), jnp.int32),
            pltpu.VMEM((PEER_TOPK, tt), F32),
        ],
        compiler_params=_params("parallel"),
        name="peer_retrieve",
    )(qt, keys2)


SUBLANES = 8
DENSE_SUB = 512


def _gate_table_kernel(g_ref, i_ref, j_ref, o_ref, *, tb, nkeys):
    sub = lax.broadcasted_iota(jnp.int32, (nkeys, g_ref.shape[1]), 0)

    def body(t, carry):
        g = g_ref[pl.ds(t, 1), :]
        i = i_ref[pl.ds(t, 1), :]
        j = j_ref[pl.ds(t, 1), :]
        a_t = jnp.where(i == sub, g, 0.0).astype(BF16)
        b_t = jnp.where(j == sub, 1.0, 0.0).astype(BF16)
        w = lax.dot_general(a_t, b_t, (((1,), (1,)), ((), ())), preferred_element_type=F32)
        o_ref[:, pl.ds(t, 1)] = w.reshape(nkeys // SUBLANES, 1, SUBLANES, nkeys)
        return carry

    lax.fori_loop(0, tb, body, 0, unroll=8)


def _gate_table(g, i, j, nkeys):
    m, p = g.shape
    tb = _divisor(m, 64, 8)
    spec = pl.BlockSpec((tb, p), lambda t: (t, 0))
    ng = nkeys // SUBLANES
    w = pl.pallas_call(
        functools.partial(_gate_table_kernel, tb=tb, nkeys=nkeys),
        grid=(m // tb,),
        in_specs=[spec, spec, spec],
        out_specs=pl.BlockSpec((ng, tb, SUBLANES, nkeys), lambda t: (0, t, 0, 0)),
        out_shape=jax.ShapeDtypeStruct((ng, m, SUBLANES, nkeys), F32),
        compiler_params=_params("parallel"),
        name="peer_gate_table",
    )(g, i, j)
    return w.reshape(ng, m * SUBLANES, nkeys)


def _peer_dense_kernel(x_ref, ut_ref, v_ref, w_ref, h_ref, o_ref, *, tm, nkeys):
    @pl.when(pl.program_id(1) == 0)
    def _():
        o_ref[...] = h_ref[...]

    per_sub = DENSE_SUB // nkeys
    for s in range(SUBLANES // per_sub):
        cols = slice(s * DENSE_SUB, (s + 1) * DENSE_SUB)
        act = jnp.dot(x_ref[...], ut_ref[:, cols], preferred_element_type=F32)
        gelu = 0.5 * act * (1.0 + lax.erf(act * (1.0 / math.sqrt(2.0))))
        w = jnp.concatenate(
            [w_ref[pl.ds(s * per_sub + i, tm, stride=SUBLANES), :] for i in range(per_sub)], axis=1)
        o_ref[...] += jnp.dot((w * gelu).astype(BF16), v_ref[cols, :], preferred_element_type=F32)


def _peer_dense(h, x, ut, v, w):
    m, d = x.shape
    ng, _, nkeys = w.shape
    assert DENSE_SUB % nkeys == 0 and SUBLANES * nkeys % DENSE_SUB == 0
    te = SUBLANES * nkeys
    tm = _divisor(m, 832, 64)
    return pl.pallas_call(
        functools.partial(_peer_dense_kernel, tm=tm, nkeys=nkeys),
        grid=(m // tm, ng),
        in_specs=[
            pl.BlockSpec((tm, d), lambda i, k: (i, 0), pipeline_mode=pl.Buffered(1)),
            pl.BlockSpec((d, te), lambda i, k: (0, k)),
            pl.BlockSpec((te, d), lambda i, k: (k, 0)),
            pl.BlockSpec((None, tm * SUBLANES, nkeys), lambda i, k: (k, i, 0)),
            pl.BlockSpec((tm, d), lambda i, k: (i, 0), pipeline_mode=pl.Buffered(1)),
        ],
        out_specs=pl.BlockSpec((tm, d), lambda i, k: (i, 0)),
        out_shape=jax.ShapeDtypeStruct((m, d), F32),
        compiler_params=_params("parallel", "arbitrary"),
        name="peer_dense",
    )(x, ut, v, w, h)


def _rms_norm(x, g):
    return x * lax.rsqrt(jnp.mean(x * x, axis=-1, keepdims=True) + NORM_EPS) * g


def _peer_layer(h, g_norm, wq, keys, u_tab, v_tab):
    bsz, tp, d = h.shape
    m = bsz * tp
    nkeys = keys.shape[2]
    xn = _rms_norm(h, g_norm).reshape(m, d).astype(BF16)
    qt = _matmul_t(wq.T.astype(BF16), xn)
    gate, ii, jj = _peer_retrieve(qt, keys)
    w = _gate_table(gate.T, ii.T, jj.T, nkeys)
    out = _peer_dense(h.reshape(m, d), xn, u_tab.T.astype(BF16), v_tab.astype(BF16), w)
    return out.reshape(bsz, tp, d)


def _rwkv_layer(h, g_norm, mix, w0, w1, w2, a0, a1, a2, g1, g2, k_k, k_a, r_k, w_rkv, w_o, lnx_w, lnx_b):
    bsz, tp, d = h.shape
    m = bsz * tp
    nh = d // HEAD
    x = _rms_norm(h, g_norm)
    xx = jnp.pad(x, ((0, 0), (1, 0), (0, 0)))[:, :-1] - x

    def mixed(n):
        return (x + xx * mix[n]).reshape(m, d).astype(BF16)

    r = _matmul(mixed(0), w_rkv[0].astype(BF16))
    k = _matmul(mixed(2), w_rkv[1].astype(BF16))
    v = _matmul(mixed(3), w_rkv[2].astype(BF16))
    w_log = -jax.nn.softplus(-_lora(mixed(1), w1, w2, w0, "tanh")) - 0.5
    lw = -jnp.exp(w_log)
    a = jax.nn.sigmoid(_lora(mixed(4), a1, a2, a0, None))
    g = _lora(mixed(5), g1, g2, None, "sigmoid")

    kk = (k * k_k).reshape(m, nh, HEAD)
    kk = (kk * lax.rsqrt(jnp.maximum(jnp.sum(kk * kk, axis=-1, keepdims=True), KK_EPS))).reshape(m, d)
    k = k * (1.0 + (a - 1.0) * k_a)
    b = kk * a

    def seq(z):
        return z.reshape(bsz, tp, d)

    y = _rwkv_scan(seq(r), seq(lw), seq(k), seq(v), seq(kk), seq(b)).reshape(m, nh, HEAD)
    mu = jnp.mean(y, axis=-1, keepdims=True)
    var = jnp.mean(jnp.square(y - mu), axis=-1, keepdims=True)
    yn = ((y - mu) * lax.rsqrt(var + GN_EPS)).reshape(m, d) * lnx_w + lnx_b
    rh = r.reshape(m, nh, HEAD)
    bonus = jnp.sum(rh * k.reshape(m, nh, HEAD) * r_k, axis=-1, keepdims=True) * v.reshape(m, nh, HEAD)
    out = (yn + bonus.reshape(m, d)) * g
    return h + _matmul(out.astype(BF16), w_o.astype(BF16)).reshape(bsz, tp, d)


def _conv_layer(h, g_norm, pw1_w, pw1_b, dw_w, dw_b, ln_w, ln_b, pw2_w, pw2_b):
    bsz, tp, d = h.shape
    m = bsz * tp
    x = _rms_norm(h, g_norm).reshape(m, d).astype(BF16)
    z = _matmul(x, pw1_w.astype(BF16), pw1_b)
    z = (z[:, :d] * jax.nn.sigmoid(z[:, d:])).reshape(bsz, tp, d)
    z = _dwconv(z, dw_w, dw_b)
    mu = jnp.mean(z, axis=-1, keepdims=True)
    var = jnp.mean(jnp.square(z - mu), axis=-1, keepdims=True)
    zn = (z - mu) * lax.rsqrt(var + LN_EPS) * ln_w + ln_b
    zn = jax.nn.silu(zn).reshape(m, d).astype(BF16)
    return h + _matmul(zn, pw2_w.astype(BF16), pw2_b).reshape(bsz, tp, d)


def kernel(x, meta_tokens, norm_mix_a, rwkv_mix, rwkv_w0, rwkv_w1, rwkv_w2, rwkv_a0, rwkv_a1, rwkv_a2, rwkv_g1, rwkv_g2, rwkv_k_k, rwkv_k_a, rwkv_r_k, rwkv_w_rkv, rwkv_w_o, rwkv_lnx_w, rwkv_lnx_b, norm_mix_b, conv_pw1_w, conv_pw1_b, conv_dw_w, conv_dw_b, conv_ln_w, conv_ln_b, conv_pw2_w, conv_pw2_b, norm_ffn, peer_wq, peer_keys, peer_u, peer_v, norm_final):
    bsz, seq, d = x.shape
    n_meta = meta_tokens.shape[0]
    t = n_meta + seq
    tp = -(-t // 128) * 128 if t <= 128 else -(-t // CHUNK) * CHUNK
    depth = norm_ffn.shape[0]
    meta = jnp.broadcast_to(meta_tokens[None].astype(x.dtype), (bsz, n_meta, d))
    h = jnp.concatenate([meta, x, jnp.zeros((bsz, tp - t, d), x.dtype)], axis=1)
    for i in range(depth):
        j = i // 2
        if i % 2 == 0:
            h = _rwkv_layer(
                h, norm_mix_a[j], rwkv_mix[j], rwkv_w0[j], rwkv_w1[j], rwkv_w2[j], rwkv_a0[j], rwkv_a1[j],
                rwkv_a2[j], rwkv_g1[j], rwkv_g2[j], rwkv_k_k[j], rwkv_k_a[j], rwkv_r_k[j], rwkv_w_rkv[j],
                rwkv_w_o[j], rwkv_lnx_w[j], rwkv_lnx_b[j])
        else:
            h = _conv_layer(
                h, norm_mix_b[j], conv_pw1_w[j], conv_pw1_b[j], conv_dw_w[j], conv_dw_b[j], conv_ln_w[j],
                conv_ln_b[j], conv_pw2_w[j], conv_pw2_b[j])
        h = _peer_layer(h, norm_ffn[i], peer_wq[i], peer_keys[i], peer_u[i], peer_v[i])
    return _rms_norm(h, norm_final)[:, n_meta:t]
```

```python
import functools
import math

import jax
import jax.numpy as jnp
from jax import lax
from jax.experimental import pallas as pl
from jax.experimental.pallas import tpu as pltpu

F32 = jnp.float32
BF16 = jnp.bfloat16

NORM_EPS = 1e-6
GN_EPS = 64e-5
KK_EPS = 1e-24
LN_EPS = 1e-5
HEAD = 64
CHUNK = 64
SUB = 16
PEER_TOPK = 16
SUBLANES = 8
VMEM_LIMIT = 56 * 1024 * 1024


def _divisor(n, target, mult):
    best = None
    for d in range(mult, min(n, target) + 1, mult):
        if n % d == 0:
            best = d
    assert best is not None, (n, target, mult)
    return best


def _params(*sem):
    return pltpu.CompilerParams(dimension_semantics=sem, vmem_limit_bytes=VMEM_LIMIT)


def _rms(x, g):
    return x * lax.rsqrt(jnp.mean(x * x, axis=-1, keepdims=True) + NORM_EPS) * g


def _proj_kernel(*refs, prologue, body, mid, post, residual, tm, seq_tiles):
    it = iter(refs)
    a_ref = next(it)
    prev_ref = next(it) if prologue == "shift" else None
    g_ref = next(it) if prologue != "none" else None
    mix_ref = next(it) if prologue == "shift" else None
    w_refs = [next(it) for _ in range({"mm": 2, "glu": 4, "lora": 3}[body])]
    res_ref = next(it) if residual else None
    o_ref = next(it)
    buf_ref = next(it) if prologue == "shift" else None

    if prologue == "none":
        a = a_ref[...]
    else:
        x = _rms(a_ref[...], g_ref[...])
        if prologue == "shift":
            first = (pl.program_id(1) % seq_tiles) == 0
            buf_ref[0:SUBLANES, :] = jnp.where(first, 0.0, _rms(prev_ref[...], g_ref[...]))
            buf_ref[SUBLANES:SUBLANES + tm, :] = x
            x = x + (buf_ref[SUBLANES - 1:SUBLANES - 1 + tm, :] - x) * mix_ref[...]
        a = x.astype(BF16)

    if body == "mm":
        w_ref, b_ref = w_refs
        z = jnp.dot(a, w_ref[...], preferred_element_type=F32) + b_ref[...]
    elif body == "glu":
        wl_ref, wr_ref, bl_ref, br_ref = w_refs
        z = ((jnp.dot(a, wl_ref[...], preferred_element_type=F32) + bl_ref[...])
             * jax.nn.sigmoid(jnp.dot(a, wr_ref[...], preferred_element_type=F32) + br_ref[...]))
    else:
        w1_ref, w2_ref, b_ref = w_refs
        hid = jnp.dot(a, w1_ref[...], preferred_element_type=F32)
        if mid == "tanh":
            hid = jnp.tanh(hid)
        elif mid == "sigmoid":
            hid = jax.nn.sigmoid(hid)
        z = jnp.dot(hid.astype(BF16), w2_ref[...], preferred_element_type=F32) + b_ref[...]
    if post == "log_decay":
        z = -jnp.exp(-jax.nn.softplus(-z) - 0.5)
    elif post == "sigmoid":
        z = jax.nn.sigmoid(z)
    if residual:
        z = z + res_ref[...]
    o_ref[...] = z.astype(o_ref.dtype)


def _proj(a, weights, *, body="mm", norm_g=None, mix=None, seq_len=None, mid=None, post=None,
          residual=None, out_dtype=F32):
    m, k = a.shape
    prologue = "none" if norm_g is None else ("norm" if mix is None else "shift")
    if body == "lora":
        w1, w2, bias = weights
        r = w1.shape[1]
        rp = -(-r // 128) * 128
        n = w2.shape[1]
        tn = n
        ws = [jnp.pad(w1, ((0, 0), (0, rp - r))).astype(BF16), jnp.pad(w2, ((0, rp - r), (0, 0))).astype(BF16)]
        w_specs = [pl.BlockSpec((k, rp), lambda j, i: (0, 0)), pl.BlockSpec((rp, n), lambda j, i: (0, 0))]
        biases = [bias]
    elif body == "glu":
        w, bias = weights
        n = w.shape[1] // 2
        tn = _divisor(n, 1024, 128)
        ws = [w[:, :n].astype(BF16), w[:, n:].astype(BF16)]
        w_specs = [pl.BlockSpec((k, tn), lambda j, i: (0, j))] * 2
        biases = [bias[:n], bias[n:]]
    else:
        w, bias = weights
        n = w.shape[1]
        tn = _divisor(n, 2048, 128)
        ws = [w.astype(BF16)]
        w_specs = [pl.BlockSpec((k, tn), lambda j, i: (0, j), pipeline_mode=pl.Buffered(1))]
        biases = [bias]
    biases = [(jnp.zeros((n,), F32) if b is None else b).reshape(1, n).astype(F32) for b in biases]
    b_specs = [pl.BlockSpec((1, tn), lambda j, i: (0, j))] * len(biases)

    rows = m if seq_len is None else seq_len
    tm = _divisor(rows, 448, 16)
    args = [a]
    specs = [pl.BlockSpec((tm, k), lambda j, i: (i, 0))]
    scratch = []
    if prologue == "shift":
        ratio = tm // SUBLANES
        args.append(a)
        specs.append(pl.BlockSpec((SUBLANES, k), lambda j, i: (jnp.maximum(i * ratio - 1, 0), 0)))
        scratch = [pltpu.VMEM((SUBLANES + tm, k), F32)]
    if prologue != "none":
        args.append(norm_g.reshape(1, k))
        specs.append(pl.BlockSpec((1, k), lambda j, i: (0, 0)))
    if prologue == "shift":
        args.append(mix.reshape(1, k))
        specs.append(pl.BlockSpec((1, k), lambda j, i: (0, 0)))
    args += ws + biases
    specs += w_specs + b_specs
    if residual is not None:
        args.append(residual)
        specs.append(pl.BlockSpec((tm, tn), lambda j, i: (i, j)))
    return pl.pallas_call(
        functools.partial(_proj_kernel, prologue=prologue, body=body, mid=mid, post=post,
                          residual=residual is not None, tm=tm, seq_tiles=rows // tm),
        grid=(n // tn, m // tm),
        in_specs=specs,
        out_specs=pl.BlockSpec((tm, tn), lambda j, i: (i, j)),
        out_shape=jax.ShapeDtypeStruct((m, n), out_dtype),
        scratch_shapes=scratch,
        compiler_params=_params("parallel", "parallel"),
        name="proj_" + prologue + "_" + body,
    )(*args)


def _proj_t_kernel(w_ref, a_ref, g_ref, o_ref):
    a = _rms(a_ref[...], g_ref[...]).astype(BF16)
    o_ref[...] = lax.dot_general(w_ref[...], a, (((1,), (1,)), ((), ())), preferred_element_type=F32)


def _proj_t(wt, a, norm_g):
    n, k = wt.shape
    m = a.shape[0]
    tm = _divisor(m, 1024, 128)
    tn = _divisor(n, 1024, 128)
    return pl.pallas_call(
        _proj_t_kernel,
        grid=(n // tn, m // tm),
        in_specs=[
            pl.BlockSpec((tn, k), lambda j, i: (j, 0)),
            pl.BlockSpec((tm, k), lambda j, i: (i, 0)),
            pl.BlockSpec((1, k), lambda j, i: (0, 0)),
        ],
        out_specs=pl.BlockSpec((tn, tm), lambda j, i: (j, i)),
        out_shape=jax.ShapeDtypeStruct((n, m), F32),
        compiler_params=_params("parallel", "parallel"),
        name="proj_t",
    )(wt, a, norm_g.reshape(1, k))


def _bdot(a, b):
    return jnp.dot(a.astype(BF16), b.astype(BF16), preferred_element_type=F32)


def _bdot_nt(a, b):
    return lax.dot_general(a.astype(BF16), b.astype(BF16), (((1,), (1,)), ((), ())),
                           preferred_element_type=F32)


def _bdot_tn(a, b):
    return lax.dot_general(a.astype(BF16), b.astype(BF16), (((0,), (0,)), ((), ())),
                           preferred_element_type=F32)


def _big(x, lo_half):
    zero = jnp.zeros_like(x)
    return jnp.concatenate([jnp.where(lo_half, x, zero), jnp.where(lo_half, zero, x)], axis=0)


def _fold(x):
    c = x.shape[0] // 2
    return x[:c] + x[c:]


def _scan_prep_kernel(r_ref, lw_ref, k_ref, v_ref, kk_ref, b_ref,
                      p_ref, m_ref, y1_ref, n_ref, dec_ref, *, npairs):
    c = CHUNK
    c2 = 2 * c
    row = lax.broadcasted_iota(jnp.int32, (c2, c2), 0)
    col = lax.broadcasted_iota(jnp.int32, (c2, c2), 1)
    strict = (row % c) > (col % c)
    incl = (row % c) >= (col % c)
    same_blk = (row // SUB) == (col // SUB)
    eye = jnp.where(row == col, 1.0, 0.0).astype(F32)
    lo_half = lax.broadcasted_iota(jnp.int32, (c, c2), 1) < HEAD
    tri = lax.broadcasted_iota(jnp.int32, (c, c), 0) >= lax.broadcasted_iota(jnp.int32, (c, c), 1)
    ones_tril = jnp.where(tri, 1.0, 0.0).astype(F32)

    lw = lw_ref[0]
    g = jnp.dot(ones_tril, lw, preferred_element_type=F32, precision=lax.Precision.HIGHEST)
    gc = g[c - 1:c, :]
    eg = jnp.exp(g)
    eng = jnp.exp(-g)
    egc = jnp.exp(gc - g)
    kk_t = kk_ref[0] * jnp.exp(g - lw)
    r_t = r_ref[0] * eg
    k_h = k_ref[0] * eng
    b_h = b_ref[0] * eng
    k_g = k_ref[0] * egc
    b_g = b_ref[0] * egc
    v_all = v_ref[0]
    dec_ref[0, 0] = jnp.exp(gc)

    units = range(npairs)

    def pair(x, u):
        return _big(x[:, u * c2:(u + 1) * c2], lo_half)

    kkt = [pair(kk_t, u) for u in units]
    rt = [pair(r_t, u) for u in units]
    vb = [pair(v_all, u) for u in units]
    bg = [pair(b_g, u) for u in units]
    gm = [_bdot_nt(jnp.concatenate([kkt[u], rt[u]], axis=0),
                   jnp.concatenate([pair(b_h, u), pair(k_h, u)], axis=0)) for u in units]
    a_b = [jnp.where(strict, gm[u][:c2, :c2], 0.0) for u in units]
    a_k = [jnp.where(strict, gm[u][:c2, c2:], 0.0) for u in units]
    a_rb = [jnp.where(incl, gm[u][c2:, :c2], 0.0) for u in units]
    a_rk = [jnp.where(incl, gm[u][c2:, c2:], 0.0) for u in units]
    x2 = [_bdot(jnp.concatenate([a_k[u], a_rk[u]], axis=0), vb[u]) for u in units]

    d = [jnp.where(same_blk, a_b[u], 0.0) for u in units]
    low = [a_b[u] - d[u] for u in units]
    p = [eye - d[u] for u in units]
    dk = d
    for _ in range(int(math.log2(SUB)) - 1):
        dk = [_bdot(dk[u], dk[u]) for u in units]
        p = [p[u] + _bdot(p[u], dk[u]) for u in units]
    e = [_bdot(p[u], low[u]) for u in units]
    q = [eye - e[u] for u in units]
    ek = e
    for _ in range(int(math.log2(CHUNK // SUB)) - 1):
        ek = [_bdot(ek[u], ek[u]) for u in units]
        q = [q[u] + _bdot(q[u], ek[u]) for u in units]
    t_inv = [_bdot(q[u], p[u]) for u in units]

    wu = [_bdot(t_inv[u], jnp.concatenate([kkt[u], x2[u][:c2]], axis=1)) for u in units]
    aw = [_bdot(a_rb[u], wu[u]) for u in units]
    m0 = [_bdot_tn(-bg[u], wu[u][:, :c2]) for u in units]
    nt = [_bdot_tn(jnp.concatenate([vb[u], wu[u][:, c2:]], axis=0),
                   jnp.concatenate([pair(k_g, u), -bg[u]], axis=0)) for u in units]
    for u in units:
        sl = slice(u * c2, (u + 1) * c2)
        p_ref[0, :, sl] = _fold(rt[u] - aw[u][:, :c2]).astype(p_ref.dtype)
        y1_ref[0, :, sl] = _fold(x2[u][c2:] - aw[u][:, c2:])
        m_ref[0, :, sl] = _fold(m0[u]).astype(m_ref.dtype)
        n_ref[0, :, sl] = _fold(nt[u])


def _scan_seq_kernel(p_ref, m_ref, y1_ref, n_ref, dec_ref, y_ref, s_ref, *, npairs):
    c2 = 2 * CHUNK

    @pl.when(pl.program_id(2) == 0)
    def _():
        s_ref[...] = jnp.zeros_like(s_ref)

    lo_half = lax.broadcasted_iota(jnp.int32, (CHUNK, c2), 1) < HEAD
    for u in range(npairs):
        sl = slice(u * c2, (u + 1) * c2)
        s = s_ref[u]
        y_ref[0, :, sl] = _fold(_bdot_nt(_big(p_ref[0, :, sl], lo_half), s)) + y1_ref[0, :, sl]
        s_ref[u] = (s * dec_ref[0, 0, :, sl] + _bdot_nt(s, _big(m_ref[0, :, sl], lo_half))
                    + _big(n_ref[0, :, sl], lo_half))


def _rwkv_scan(r, lw, k, v, kk, b):
    bsz, tp, d = r.shape
    nc = tp // CHUNK
    lanes = _divisor(d, 1024, 128)
    spec = pl.BlockSpec((1, CHUNK, lanes), lambda i, j, c: (i, c, j))
    dspec = pl.BlockSpec((1, 1, 1, lanes), lambda i, j, c: (i, c, 0, j))
    tile =jax.ShapeDtypeStruct((bsz, tp, d), F32)
    tile16 = jax.ShapeDtypeStruct((bsz, tp, d), BF16)
    p, m0, y1, nt, dec = pl.pallas_call(
        functools.partial(_scan_prep_kernel, npairs=lanes // 128),
        grid=(bsz, d // lanes, nc),
        in_specs=[spec] * 6,
        out_specs=[spec, spec, spec, spec, dspec],
        out_shape=[tile16, tile16, tile, tile, jax.ShapeDtypeStruct((bsz, nc, 1, d), F32)],
        compiler_params=_params("parallel", "parallel", "parallel"),
        name="rwkv_scan_prep",
    )(r, lw, k, v, kk, b)
    lanes = _divisor(d, 1024, 128)
    spec = pl.BlockSpec((1, CHUNK, lanes), lambda i, j, c: (i, c, j))
    dspec = pl.BlockSpec((1, 1, 1, lanes), lambda i, j, c: (i, c, 0, j))
    return pl.pallas_call(
        functools.partial(_scan_seq_kernel, npairs=lanes // 128),
        grid=(bsz, d // lanes, nc),
        in_specs=[spec, spec, spec, spec, dspec],
        out_specs=spec,
        out_shape=tile,
        scratch_shapes=[pltpu.VMEM((lanes // 128, 2 * CHUNK, 2 * CHUNK), F32)],
        compiler_params=_params("parallel", "parallel", "arbitrary"),
        name="rwkv_scan_seq",
    )(p, m0, y1, nt, dec)


HALO = 32


CONV_STRIP = 512


def _conv_kernel(prev_ref, cur_ref, w_ref, b_ref, lnw_ref, lnb_ref, o_ref, buf_ref, acc_ref, sh_ref, *, width, tt, strip):
    first = pl.program_id(1) == 0
    buf_ref[0:HALO, :] = jnp.where(first, 0.0, prev_ref[0])
    buf_ref[HALO:HALO + tt, :] = cur_ref[0]
    off = HALO - (width - 1)
    for s in range(buf_ref.shape[1] // strip):
        lanes = slice(s * strip, (s + 1) * strip)
        acc = jnp.zeros((tt, strip), F32) + b_ref[:, lanes]
        for res in range(SUBLANES):
            span = tt + HALO - (SUBLANES if res else 0)
            if res:
                sh_ref[0:span, :] = buf_ref[res:res + span, lanes]
            for base in range(0, span - tt + 1, SUBLANES):
                j = base + res - off
                if 0 <= j < width:
                    src = sh_ref[base:base + tt, :] if res else buf_ref[base:base + tt, lanes]
                    acc = acc + src * w_ref[j:j + 1, lanes]
        acc_ref[:, lanes] = acc
    z = acc_ref[...]
    mu = jnp.mean(z, axis=-1, keepdims=True)
    zc = z - mu
    var = jnp.mean(zc * zc, axis=-1, keepdims=True)
    zn = zc * lax.rsqrt(var + LN_EPS) * lnw_ref[...] + lnb_ref[...]
    o_ref[0] = (zn * jax.nn.sigmoid(zn)).astype(o_ref.dtype)


def _dwconv_ln_swish(x, w, bias, ln_w, ln_b):
    bsz, tp, d = x.shape
    width = w.shape[0]
    assert width - 1 <= HALO
    tt = _divisor(tp, 320, HALO)
    ratio = tt // HALO
    wpad = jnp.pad(w, ((0, HALO - width), (0, 0)))
    row = pl.BlockSpec((1, d), lambda i, t: (0, 0))
    strip = _divisor(d, CONV_STRIP, 128)
    return pl.pallas_call(
        functools.partial(_conv_kernel, width=width, tt=tt, strip=strip),
        grid=(bsz, tp // tt),
        in_specs=[
            pl.BlockSpec((1, HALO, d), lambda i, t: (i, jnp.maximum(t * ratio - 1, 0), 0)),
            pl.BlockSpec((1, tt, d), lambda i, t: (i, t, 0)),
            pl.BlockSpec((HALO, d), lambda i, t: (0, 0)),
            row, row, row,
        ],
        out_specs=pl.BlockSpec((1, tt, d), lambda i, t: (i, t, 0)),
        out_shape=jax.ShapeDtypeStruct((bsz, tp, d), BF16),
        scratch_shapes=[pltpu.VMEM((HALO + tt, d), F32), pltpu.VMEM((tt, d), F32),
                        pltpu.VMEM((HALO + tt, strip), F32)],
        compiler_params=_params("parallel", "parallel"),
        name="dwconv_ln_swish",
    )(x, x, wpad, bias.reshape(1, d), ln_w.reshape(1, d), ln_b.reshape(1, d))


NEG = -jnp.inf
N_CAND = 16 + 7 * 8 + 8


def _cand_static():
    row = lax.broadcasted_iota(jnp.int32, (N_CAND, 1), 0)
    grp = jnp.where(row < 16, 0, (row - 16) // 8 + 1)
    bmid = (row - 16) % 8
    last = row >= 72
    a = jnp.where(row < 16, 0, jnp.where(last, row - 64, grp))
    b = jnp.where(row < 16, row, jnp.where(last, 0, bmid))
    valid = (a + 1) * (b + 1) <= PEER_TOPK
    return a * PEER_TOPK + b, valid


def _retrieve_kernel(q_ref, keys_ref, g_ref, i_ref, j_ref,
                     sv_ref, si_ref, cs_ref, cc_ref, ts_ref, tc_ref, *, nheads, nkeys, tt):
    kiota = lax.broadcasted_iota(jnp.int32, (nkeys, tt), 0).astype(F32)
    flat, valid = _cand_static()
    code0 = (flat * (nkeys * nkeys)).astype(F32)
    big = float(1 << 24)

    def head_body(h, carry):
        for half in range(2):
            hc = h * 2 + half
            q = q_ref[pl.ds(pl.multiple_of(hc * nkeys, nkeys), nkeys), :]
            s = jnp.dot(keys_ref[hc], q.astype(BF16), preferred_element_type=F32)
            for r in range(PEER_TOPK):
                m = jnp.max(s, axis=0, keepdims=True)
                idx = jnp.min(jnp.where(s == m, kiota, big), axis=0, keepdims=True)
                sv_ref[half, r:r + 1, :] = m
                si_ref[half, r:r + 1, :] = idx
                s = jnp.where(kiota == idx, NEG, s)
        cs_ref[0:16, :] = sv_ref[0, 0:1, :] + sv_ref[1, :, :]
        cc_ref[0:16, :] = si_ref[0, 0:1, :] * nkeys + si_ref[1, :, :]
        for a in range(1, 8):
            lo = 16 + (a - 1) * 8
            cs_ref[lo:lo + 8, :] = sv_ref[0, a:a + 1, :] + sv_ref[1, 0:8, :]
            cc_ref[lo:lo + 8, :] = si_ref[0, a:a + 1, :] * nkeys + si_ref[1, 0:8, :]
        cs_ref[72:80, :] = sv_ref[0, 8:16, :] + sv_ref[1, 0:1, :]
        cc_ref[72:80, :] = si_ref[0, 8:16, :] * nkeys + si_ref[1, 0:1, :]
        cand = jnp.where(valid, cs_ref[...], NEG)
        code = cc_ref[...] + code0
        for r in range(PEER_TOPK):
            m = jnp.max(cand, axis=0, keepdims=True)
            c = jnp.min(jnp.where(cand == m, code, big), axis=0, keepdims=True)
            ts_ref[r:r + 1, :] = m
            tc_ref[r:r + 1, :] = c
            cand = jnp.where(code == c, NEG, cand)
        base = pl.multiple_of(h * PEER_TOPK, PEER_TOPK)
        top = ts_ref[...]
        ex = jnp.exp(top - jnp.max(top, axis=0, keepdims=True))
        g_ref[pl.ds(base, PEER_TOPK), :] = ex / jnp.sum(ex, axis=0, keepdims=True)
        expert = tc_ref[...].astype(jnp.int32) % (nkeys * nkeys)
        i_ref[pl.ds(base, PEER_TOPK), :] = expert // nkeys
        j_ref[pl.ds(base, PEER_TOPK), :] = expert % nkeys
        return carry

    lax.fori_loop(0, nheads, head_body, 0)


def _peer_retrieve(qt, keys):
    ph, _, nkeys, dk = keys.shape
    m = qt.shape[1]
    assert nkeys == dk == 128
    tt = _divisor(m, 256, 128)
    keys2 = keys.reshape(ph * 2, nkeys, dk).astype(BF16)
    np_ = ph * PEER_TOPK
    out_spec = pl.BlockSpec((np_, tt), lambda i: (0, i))
    return pl.pallas_call(
        functools.partial(_retrieve_kernel, nheads=ph, nkeys=nkeys, tt=tt),
        grid=(m // tt,),
        in_specs=[
            pl.BlockSpec((ph * 2 * dk, tt), lambda i: (0, i)),
            pl.BlockSpec((ph * 2, nkeys, dk), lambda i: (0, 0, 0)),
        ],
        out_specs=[out_spec, out_spec, out_spec],
        out_shape=[
            jax.ShapeDtypeStruct((np_, m), F32),
            jax.ShapeDtypeStruct((np_, m), jnp.int32),
            jax.ShapeDtypeStruct((np_, m), jnp.int32),
        ],
        scratch_shapes=[
            pltpu.VMEM((2, PEER_TOPK, tt), F32),
            pltpu.VMEM((2, PEER_TOPK, tt), F32),
            pltpu.VMEM((N_CAND, tt), F32),
            pltpu.VMEM((N_CAND, tt), F32),
            pltpu.VMEM((PEER_TOPK, tt), F32),
            pltpu.VMEM((PEER_TOPK, tt), F32),
        ],
        compiler_params=_params("parallel"),
        name="peer_retrieve",
    )(qt, keys2)


DENSE_SUB = 512


def _gate_table_kernel(g_ref, i_ref, j_ref, o_ref, *, tb, nkeys):
    sub = lax.broadcasted_iota(jnp.int32, (nkeys, g_ref.shape[1]), 0)

    def body(t, carry):
        g = g_ref[pl.ds(t, 1), :]
        i = i_ref[pl.ds(t, 1), :]
        j = j_ref[pl.ds(t, 1), :]
        a_t = jnp.where(i == sub, g, 0.0).astype(BF16)
        b_t = jnp.where(j == sub, 1.0, 0.0).astype(BF16)
        w = lax.dot_general(a_t, b_t, (((1,), (1,)), ((), ())), preferred_element_type=F32)
        o_ref[:, pl.ds(t, 1)] = w.reshape(nkeys // SUBLANES, 1, SUBLANES, nkeys)
        return carry

    lax.fori_loop(0, tb, body, 0, unroll=8)


def _gate_table(g, i, j, nkeys):
    m, p = g.shape
    tb = _divisor(m, 64, 8)
    spec = pl.BlockSpec((tb, p), lambda t: (t, 0))
    ng = nkeys // SUBLANES
    w = pl.pallas_call(
        functools.partial(_gate_table_kernel, tb=tb, nkeys=nkeys),
        grid=(m // tb,),
        in_specs=[spec, spec, spec],
        out_specs=pl.BlockSpec((ng, tb, SUBLANES, nkeys), lambda t: (0, t, 0, 0)),
        out_shape=jax.ShapeDtypeStruct((ng, m, SUBLANES, nkeys), F32),
        compiler_params=_params("parallel"),
        name="peer_gate_table",
    )(g, i, j)
    return w.reshape(ng, m * SUBLANES, nkeys)


def _peer_dense_kernel(*refs, tm, nkeys, final_norm):
    if final_norm:
        h_ref, g_ref, ut_ref, v_ref, w_ref, gf_ref, o_ref, x_ref = refs
    else:
        h_ref, g_ref, ut_ref, v_ref, w_ref, o_ref, x_ref = refs
    k = pl.program_id(1)

    @pl.when(k == 0)
    def _():
        h = h_ref[...]
        o_ref[...] = h
        x_ref[...] = _rms(h, g_ref[...]).astype(BF16)

    per_sub = DENSE_SUB // nkeys
    for s in range(SUBLANES // per_sub):
        cols = slice(s * DENSE_SUB, (s + 1) * DENSE_SUB)
        act = jnp.dot(x_ref[...], ut_ref[:, cols], preferred_element_type=F32)
        gelu = 0.5 * act * (1.0 + lax.erf(act * (1.0 / math.sqrt(2.0))))
        w = jnp.concatenate(
            [w_ref[pl.ds(s * per_sub + i, tm, stride=SUBLANES), :] for i in range(per_sub)], axis=1)
        o_ref[...] += jnp.dot((w * gelu).astype(BF16), v_ref[cols, :], preferred_element_type=F32)

    if final_norm:
        @pl.when(k == pl.num_programs(1) - 1)
        def _():
            o_ref[...] = _rms(o_ref[...], gf_ref[...])


def _peer_dense(h, norm_g, ut, v, w, final_g=None):
    m, d = h.shape
    ng, _, nkeys = w.shape
    assert DENSE_SUB % nkeys == 0 and SUBLANES * nkeys % DENSE_SUB == 0
    te = SUBLANES * nkeys
    tm = _divisor(m, 832, 64)
    row = pl.BlockSpec((1, d), lambda i, k: (0, 0))
    args = [h, norm_g.reshape(1, d), ut, v, w]
    specs = [
        pl.BlockSpec((tm, d), lambda i, k: (i, 0), pipeline_mode=pl.Buffered(1)),
        row,
        pl.BlockSpec((d, te), lambda i, k: (0, k)),
        pl.BlockSpec((te, d), lambda i, k: (k, 0)),
        pl.BlockSpec((None, tm * SUBLANES, nkeys), lambda i, k: (k, i, 0)),
    ]
    if final_g is not None:
        args.append(final_g.reshape(1, d))
        specs.append(row)
    return pl.pallas_call(
        functools.partial(_peer_dense_kernel, tm=tm, nkeys=nkeys, final_norm=final_g is not None),
        grid=(m // tm, ng),
        in_specs=specs,
        out_specs=pl.BlockSpec((tm, d), lambda i, k: (i, 0)),
        out_shape=jax.ShapeDtypeStruct((m, d), F32),
        scratch_shapes=[pltpu.VMEM((tm, d), BF16)],
        compiler_params=_params("parallel", "arbitrary"),
        name="peer_dense",
    )(*args)


def _peer_layer(h, g_norm, wq, keys, u_tab, v_tab, final_g=None):
    nkeys = keys.shape[2]
    qt = _proj_t(wq.T.astype(BF16), h, g_norm)
    gate, ii, jj = _peer_retrieve(qt, keys)
    w = _gate_table(gate.T, ii.T, jj.T, nkeys)
    return _peer_dense(h, g_norm, u_tab.T.astype(BF16), v_tab.astype(BF16), w, final_g)


def _rwkv_layer(h, tp, g_norm, mix, w0, w1, w2, a0, a1, a2, g1, g2, k_k, k_a, r_k, w_rkv, w_o, lnx_w, lnx_b):
    m, d = h.shape
    bsz = m // tp
    nh = d // HEAD

    def shifted(n, weights, **kw):
        return _proj(h, weights, norm_g=g_norm, mix=mix[n], seq_len=tp, **kw)

    r = shifted(0, (w_rkv[0], None))
    k = shifted(2, (w_rkv[1], None))
    v = shifted(3, (w_rkv[2], None))
    lw = shifted(1, (w1, w2, w0), body="lora", mid="tanh", post="log_decay")
    a = shifted(4, (a1, a2, a0), body="lora", post="sigmoid")
    g = shifted(5, (g1, g2, None), body="lora", mid="sigmoid")

    kk = (k * k_k).reshape(m, nh, HEAD)
    kk = (kk * lax.rsqrt(jnp.maximum(jnp.sum(kk * kk, axis=-1, keepdims=True), KK_EPS))).reshape(m, d)
    k = k * (1.0 + (a - 1.0) * k_a)
    b = kk * a

    def seq(z):
        return z.reshape(bsz, tp, d)

    y = _rwkv_scan(seq(r), seq(lw), seq(k), seq(v), seq(kk), seq(b)).reshape(m, nh, HEAD)
    mu = jnp.mean(y, axis=-1, keepdims=True)
    var = jnp.mean(jnp.square(y - mu), axis=-1, keepdims=True)
    yn = ((y - mu) * lax.rsqrt(var + GN_EPS)).reshape(m, d) * lnx_w + lnx_b
    rh = r.reshape(m, nh, HEAD)
    bonus = jnp.sum(rh * k.reshape(m, nh, HEAD) * r_k, axis=-1, keepdims=True) * v.reshape(m, nh, HEAD)
    out = (yn + bonus.reshape(m, d)) * g
    return _proj(out.astype(BF16), (w_o, None), residual=h)


def _conv_layer(h, tp, g_norm, pw1_w, pw1_b, dw_w, dw_b, ln_w, ln_b, pw2_w, pw2_b):
    m, d = h.shape
    z = _proj(h, (pw1_w, pw1_b), body="glu", norm_g=g_norm)
    z = _dwconv_ln_swish(z.reshape(m // tp, tp, d), dw_w, dw_b, ln_w, ln_b).reshape(m, d)
    return _proj(z, (pw2_w, pw2_b), residual=h)


def kernel(x, meta_tokens, norm_mix_a, rwkv_mix, rwkv_w0, rwkv_w1, rwkv_w2, rwkv_a0, rwkv_a1, rwkv_a2, rwkv_g1, rwkv_g2, rwkv_k_k, rwkv_k_a, rwkv_r_k, rwkv_w_rkv, rwkv_w_o, rwkv_lnx_w, rwkv_lnx_b, norm_mix_b, conv_pw1_w, conv_pw1_b, conv_dw_w, conv_dw_b, conv_ln_w, conv_ln_b, conv_pw2_w, conv_pw2_b, norm_ffn, peer_wq, peer_keys, peer_u, peer_v, norm_final):
    bsz, seq, d = x.shape
    n_meta = meta_tokens.shape[0]
    t = n_meta + seq
    tp = -(-t // 128) * 128 if t <= 128 else -(-t // CHUNK) * CHUNK
    depth = norm_ffn.shape[0]
    meta = jnp.broadcast_to(meta_tokens[None].astype(x.dtype), (bsz, n_meta, d))
    h = jnp.concatenate([meta, x, jnp.zeros((bsz, tp - t, d), x.dtype)], axis=1).reshape(bsz * tp, d)
    for i in range(depth):
        j = i // 2
        if i % 2 == 0:
            h = _rwkv_layer(
                h, tp, norm_mix_a[j], rwkv_mix[j], rwkv_w0[j], rwkv_w1[j], rwkv_w2[j], rwkv_a0[j], rwkv_a1[j],
                rwkv_a2[j], rwkv_g1[j], rwkv_g2[j], rwkv_k_k[j], rwkv_k_a[j], rwkv_r_k[j], rwkv_w_rkv[j],
                rwkv_w_o[j], rwkv_lnx_w[j], rwkv_lnx_b[j])
        else:
            h = _conv_layer(
                h, tp, norm_mix_b[j], conv_pw1_w[j], conv_pw1_b[j], conv_dw_w[j], conv_dw_b[j], conv_ln_w[j],
                conv_ln_b[j], conv_pw2_w[j], conv_pw2_b[j])
        h = _peer_layer(h, norm_ffn[i], peer_wq[i], peer_keys[i], peer_u[i], peer_v[i],
                        norm_final if i == depth - 1 else None)
    if depth == 0:
        h = _rms(h, norm_final)
    return h.reshape(bsz, tp, d)[:, n_meta:t]
```

```python
import functools
import math

import jax
import jax.numpy as jnp
from jax import lax
from jax.experimental import pallas as pl
from jax.experimental.pallas import tpu as pltpu

F32 = jnp.float32
BF16 = jnp.bfloat16

NORM_EPS = 1e-6
GN_EPS = 64e-5
KK_EPS = 1e-24
LN_EPS = 1e-5
HEAD = 64
CHUNK = 64
SUB = 16
PEER_TOPK = 16
SUBLANES = 8
VMEM_LIMIT = 56 * 1024 * 1024


def _divisor(n, target, mult):
    best = None
    for d in range(mult, min(n, target) + 1, mult):
        if n % d == 0:
            best = d
    assert best is not None, (n, target, mult)
    return best


def _params(*sem):
    return pltpu.CompilerParams(dimension_semantics=sem, vmem_limit_bytes=VMEM_LIMIT)


def _rms(x, g):
    return x * lax.rsqrt(jnp.mean(x * x, axis=-1, keepdims=True) + NORM_EPS) * g


def _proj_kernel(*refs, prologue, body, mid, post, residual, tm, seq_tiles):
    it = iter(refs)
    a_ref = next(it)
    prev_ref = next(it) if prologue == "shift" else None
    g_ref = next(it) if prologue != "none" else None
    mix_ref = next(it) if prologue == "shift" else None
    w_refs = [next(it) for _ in range({"mm": 2, "glu": 4, "lora": 3}[body])]
    res_ref = next(it) if residual else None
    o_ref = next(it)
    buf_ref = next(it) if prologue == "shift" else None

    if prologue == "none":
        a = a_ref[...]
    else:
        x = _rms(a_ref[...], g_ref[...])
        if prologue == "shift":
            first = (pl.program_id(1) % seq_tiles) == 0
            buf_ref[0:SUBLANES, :] = jnp.where(first, 0.0, _rms(prev_ref[...], g_ref[...]))
            buf_ref[SUBLANES:SUBLANES + tm, :] = x
            x = x + (buf_ref[SUBLANES - 1:SUBLANES - 1 + tm, :] - x) * mix_ref[...]
        a = x.astype(BF16)

    if body == "mm":
        w_ref, b_ref = w_refs
        z = jnp.dot(a, w_ref[...], preferred_element_type=F32) + b_ref[...]
    elif body == "glu":
        wl_ref, wr_ref, bl_ref, br_ref = w_refs
        z = ((jnp.dot(a, wl_ref[...], preferred_element_type=F32) + bl_ref[...])
             * jax.nn.sigmoid(jnp.dot(a, wr_ref[...], preferred_element_type=F32) + br_ref[...]))
    else:
        w1_ref, w2_ref, b_ref = w_refs
        hid = jnp.dot(a, w1_ref[...], preferred_element_type=F32)
        if mid == "tanh":
            hid = jnp.tanh(hid)
        elif mid == "sigmoid":
            hid = jax.nn.sigmoid(hid)
        z = jnp.dot(hid.astype(BF16), w2_ref[...], preferred_element_type=F32) + b_ref[...]
    if post == "log_decay":
        z = -jnp.exp(-jax.nn.softplus(-z) - 0.5)
    elif post == "sigmoid":
        z = jax.nn.sigmoid(z)
    if residual:
        z = z + res_ref[...]
    o_ref[...] = z.astype(o_ref.dtype)


def _proj(a, weights, *, body="mm", norm_g=None, mix=None, seq_len=None, mid=None, post=None,
          residual=None, out_dtype=F32):
    m, k = a.shape
    prologue = "none" if norm_g is None else ("norm" if mix is None else "shift")
    if body == "lora":
        w1, w2, bias = weights
        r = w1.shape[1]
        rp = -(-r // 128) * 128
        n = w2.shape[1]
        tn = n
        ws = [jnp.pad(w1, ((0, 0), (0, rp - r))).astype(BF16), jnp.pad(w2, ((0, rp - r), (0, 0))).astype(BF16)]
        w_specs = [pl.BlockSpec((k, rp), lambda j, i: (0, 0)), pl.BlockSpec((rp, n), lambda j, i: (0, 0))]
        biases = [bias]
    elif body == "glu":
        w, bias = weights
        n = w.shape[1] // 2
        tn = _divisor(n, 1024, 128)
        ws = [w[:, :n].astype(BF16), w[:, n:].astype(BF16)]
        w_specs = [pl.BlockSpec((k, tn), lambda j, i: (0, j))] * 2
        biases = [bias[:n], bias[n:]]
    else:
        w, bias = weights
        n = w.shape[1]
        tn = _divisor(n, 2048, 128)
        ws = [w.astype(BF16)]
        w_specs = [pl.BlockSpec((k, tn), lambda j, i: (0, j), pipeline_mode=pl.Buffered(1))]
        biases = [bias]
    biases = [(jnp.zeros((n,), F32) if b is None else b).reshape(1, n).astype(F32) for b in biases]
    b_specs = [pl.BlockSpec((1, tn), lambda j, i: (0, j))] * len(biases)

    rows = m if seq_len is None else seq_len
    tm = _divisor(rows, 448, 16)
    args = [a]
    specs = [pl.BlockSpec((tm, k), lambda j, i: (i, 0))]
    scratch = []
    if prologue == "shift":
        ratio = tm // SUBLANES
        args.append(a)
        specs.append(pl.BlockSpec((SUBLANES, k), lambda j, i: (jnp.maximum(i * ratio - 1, 0), 0)))
        scratch = [pltpu.VMEM((SUBLANES + tm, k), F32)]
    if prologue != "none":
        args.append(norm_g.reshape(1, k))
        specs.append(pl.BlockSpec((1, k), lambda j, i: (0, 0)))
    if prologue == "shift":
        args.append(mix.reshape(1, k))
        specs.append(pl.BlockSpec((1, k), lambda j, i: (0, 0)))
    args += ws + biases
    specs += w_specs + b_specs
    if residual is not None:
        args.append(residual)
        specs.append(pl.BlockSpec((tm, tn), lambda j, i: (i, j)))
    return pl.pallas_call(
        functools.partial(_proj_kernel, prologue=prologue, body=body, mid=mid, post=post,
                          residual=residual is not None, tm=tm, seq_tiles=rows // tm),
        grid=(n // tn, m // tm),
        in_specs=specs,
        out_specs=pl.BlockSpec((tm, tn), lambda j, i: (i, j)),
        out_shape=jax.ShapeDtypeStruct((m, n), out_dtype),
        scratch_shapes=scratch,
        compiler_params=_params("parallel", "parallel"),
        name="proj_" + prologue + "_" + body,
    )(*args)


def _proj_t_kernel(w_ref, a_ref, g_ref, o_ref):
    a = _rms(a_ref[...], g_ref[...]).astype(BF16)
    o_ref[...] = lax.dot_general(w_ref[...], a, (((1,), (1,)), ((), ())), preferred_element_type=F32)


def _proj_t(wt, a, norm_g):
    n, k = wt.shape
    m = a.shape[0]
    tm = _divisor(m, 1024, 128)
    tn = _divisor(n, 1024, 128)
    return pl.pallas_call(
        _proj_t_kernel,
        grid=(n // tn, m // tm),
        in_specs=[
            pl.BlockSpec((tn, k), lambda j, i: (j, 0)),
            pl.BlockSpec((tm, k), lambda j, i: (i, 0)),
            pl.BlockSpec((1, k), lambda j, i: (0, 0)),
        ],
        out_specs=pl.BlockSpec((tn, tm), lambda j, i: (j, i)),
        out_shape=jax.ShapeDtypeStruct((n, m), F32),
        compiler_params=_params("parallel", "parallel"),
        name="proj_t",
    )(wt, a, norm_g.reshape(1, k))


def _bdot(a, b):
    return jnp.dot(a.astype(BF16), b.astype(BF16), preferred_element_type=F32)


def _bdot_nt(a, b):
    return lax.dot_general(a.astype(BF16), b.astype(BF16), (((1,), (1,)), ((), ())),
                           preferred_element_type=F32)


def _bdot_tn(a, b):
    return lax.dot_general(a.astype(BF16), b.astype(BF16), (((0,), (0,)), ((), ())),
                           preferred_element_type=F32)


def _big(x, lo_half):
    zero = jnp.zeros_like(x)
    return jnp.concatenate([jnp.where(lo_half, x, zero), jnp.where(lo_half, zero, x)], axis=0)


def _fold(x):
    c = x.shape[0] // 2
    return x[:c] + x[c:]


def _scan_prep_kernel(r_ref, lw_ref, k_ref, v_ref, a_ref, kkw_ref, kaw_ref, rkw_ref,
                      p_ref, m_ref, y1_ref, n_ref, dec_ref, bonus_ref, *, npairs):
    c = CHUNK
    c2 = 2 * c
    row = lax.broadcasted_iota(jnp.int32, (c2, c2), 0)
    col = lax.broadcasted_iota(jnp.int32, (c2, c2), 1)
    strict = (row % c) > (col % c)
    incl = (row % c) >= (col % c)
    same_blk = (row // SUB) == (col // SUB)
    eye = jnp.where(row == col, 1.0, 0.0).astype(F32)
    lo_half = lax.broadcasted_iota(jnp.int32, (c, c2), 1) < HEAD
    tri = lax.broadcasted_iota(jnp.int32, (c, c), 0) >= lax.broadcasted_iota(jnp.int32, (c, c), 1)
    ones_tril = jnp.where(tri, 1.0, 0.0).astype(F32)

    lw = lw_ref[0]
    g = jnp.dot(ones_tril, lw, preferred_element_type=F32, precision=lax.Precision.HIGHEST)
    gc = g[c - 1:c, :]
    eg = jnp.exp(g)
    eng = jnp.exp(-g)
    egc = jnp.exp(gc - g)
    r_all = r_ref[0]
    a_all = a_ref[0]
    v_all = v_ref[0]
    kkr = k_ref[0] * kkw_ref[...]
    k_all = k_ref[0] * (1.0 + (a_all - 1.0) * kaw_ref[...])
    kka = kkr * a_all
    kk_t = kkr * jnp.exp(g - lw)
    r_t = r_all * eg
    k_h = k_all * eng
    b_h = kka * eng
    k_g = k_all * egc
    b_g = kka * egc
    rk = r_all * k_all * rkw_ref[...]
    dec_ref[0, 0] = jnp.exp(gc)

    units = range(npairs)

    def pair(x, u):
        return _big(x[:, u * c2:(u + 1) * c2], lo_half)

    kk_rs = [lax.rsqrt(jnp.maximum(jnp.sum(jnp.square(pair(kkr, u)), axis=1, keepdims=True), KK_EPS))
             for u in units]
    kkt = [pair(kk_t, u) * kk_rs[u] for u in units]
    rt = [pair(r_t, u) for u in units]
    vb = [pair(v_all, u) for u in units]
    bg = [pair(b_g, u) * kk_rs[u] for u in units]
    for u in units:
        bonus_ref[0, :, u * c2:(u + 1) * c2] = _fold(jnp.sum(pair(rk, u), axis=1, keepdims=True) * vb[u])
    gm = [_bdot_nt(jnp.concatenate([kkt[u], rt[u]], axis=0),
                   jnp.concatenate([pair(b_h, u) * kk_rs[u], pair(k_h, u)], axis=0)) for u in units]
    a_b = [jnp.where(strict, gm[u][:c2, :c2], 0.0) for u in units]
    a_k = [jnp.where(strict, gm[u][:c2, c2:], 0.0) for u in units]
    a_rb = [jnp.where(incl, gm[u][c2:, :c2], 0.0) for u in units]
    a_rk = [jnp.where(incl, gm[u][c2:, c2:], 0.0) for u in units]
    x2 = [_bdot(jnp.concatenate([a_k[u], a_rk[u]], axis=0), vb[u]) for u in units]

    d = [jnp.where(same_blk, a_b[u], 0.0) for u in units]
    low = [a_b[u] - d[u] for u in units]
    p = [eye - d[u] for u in units]
    dk = d
    for _ in range(int(math.log2(SUB)) - 1):
        dk = [_bdot(dk[u], dk[u]) for u in units]
        p = [p[u] + _bdot(p[u], dk[u]) for u in units]
    e = [_bdot(p[u], low[u]) for u in units]
    q = [eye - e[u] for u in units]
    ek = e
    for _ in range(int(math.log2(CHUNK // SUB)) - 1):
        ek = [_bdot(ek[u], ek[u]) for u in units]
        q = [q[u] + _bdot(q[u], ek[u]) for u in units]
    t_inv = [_bdot(q[u], p[u]) for u in units]

    wu = [_bdot(t_inv[u], jnp.concatenate([kkt[u], x2[u][:c2]], axis=1)) for u in units]
    aw = [_bdot(a_rb[u], wu[u]) for u in units]
    m0 = [_bdot_tn(-bg[u], wu[u][:, :c2]) for u in units]
    nt = [_bdot_tn(jnp.concatenate([vb[u], wu[u][:, c2:]], axis=0),
                   jnp.concatenate([pair(k_g, u), -bg[u]], axis=0)) for u in units]
    for u in units:
        sl = slice(u * c2, (u + 1) * c2)
        p_ref[0, :, sl] = _fold(rt[u] - aw[u][:, :c2]).astype(p_ref.dtype)
        y1_ref[0, :, sl] = _fold(x2[u][c2:] - aw[u][:, c2:])
        m_ref[0, :, sl] = _fold(m0[u]).astype(m_ref.dtype)
        n_ref[0, :, sl] = _fold(nt[u])


def _scan_seq_kernel(p_ref, m_ref, y1_ref, n_ref, dec_ref, bonus_ref, g_ref, lnw_ref, lnb_ref,
                     o_ref, s_ref, *, npairs):
    c2 = 2 * CHUNK

    @pl.when(pl.program_id(2) == 0)
    def _():
        s_ref[...] = jnp.zeros_like(s_ref)

    lo_half = lax.broadcasted_iota(jnp.int32, (CHUNK, c2), 1) < HEAD
    own = ((lax.broadcasted_iota(jnp.int32, (c2, c2), 0) < CHUNK)
           == (lax.broadcasted_iota(jnp.int32, (c2, c2), 1) < HEAD))
    for u in range(npairs):
        sl = slice(u * c2, (u + 1) * c2)
        s = s_ref[u]
        y = _bdot_nt(_big(p_ref[0, :, sl], lo_half), s) + _big(y1_ref[0, :, sl], lo_half)
        s_ref[u] = (s * dec_ref[0, 0, :, sl] + _bdot_nt(s, _big(m_ref[0, :, sl], lo_half))
                    + _big(n_ref[0, :, sl], lo_half))
        mu = jnp.sum(y, axis=1, keepdims=True) * (1.0 / HEAD)
        yc = jnp.where(own, y - mu, 0.0)
        var = jnp.sum(yc * yc, axis=1, keepdims=True) * (1.0 / HEAD)
        yn = _fold(yc * lax.rsqrt(var + GN_EPS)) * lnw_ref[:, sl] + lnb_ref[:, sl]
        o_ref[0, :, sl] = ((yn + bonus_ref[0, :, sl]) * g_ref[0, :, sl]).astype(o_ref.dtype)


def _rwkv_mix(r, lw, k, v, a, g, k_k, k_a, r_k, lnx_w, lnx_b):
    bsz, tp, d = r.shape
    nc = tp // CHUNK
    lanes = _divisor(d, 1024, 128)
    spec = pl.BlockSpec((1, CHUNK, lanes), lambda i, j, c: (i, c, j))
    dspec = pl.BlockSpec((1, 1, 1, lanes), lambda i, j, c: (i, c, 0, j))
    rowspec = pl.BlockSpec((1, lanes), lambda i, j, c: (0, j))
    tile = jax.ShapeDtypeStruct((bsz, tp, d), F32)
    tile16 = jax.ShapeDtypeStruct((bsz, tp, d), BF16)
    p, m0, y1, nt, dec, bonus = pl.pallas_call(
        functools.partial(_scan_prep_kernel, npairs=lanes // 128),
        grid=(bsz, d // lanes, nc),
        in_specs=[spec] * 5 + [rowspec] * 3,
        out_specs=[spec, spec, spec, spec, dspec, spec],
        out_shape=[tile16, tile16, tile, tile, jax.ShapeDtypeStruct((bsz, nc, 1, d), F32), tile],
        compiler_params=_params("parallel", "parallel", "parallel"),
        name="rwkv_scan_prep",
    )(r, lw, k, v, a, k_k.reshape(1, d), k_a.reshape(1, d), r_k.reshape(1, d))
    lanes = _divisor(d, 2048, 128)
    spec = pl.BlockSpec((1, CHUNK, lanes), lambda i, j, c: (i, c, j))
    dspec = pl.BlockSpec((1, 1, 1, lanes), lambda i, j, c: (i, c, 0, j))
    rowspec = pl.BlockSpec((1, lanes), lambda i, j, c: (0, j))
    return pl.pallas_call(
        functools.partial(_scan_seq_kernel, npairs=lanes // 128),
        grid=(bsz, d // lanes, nc),
        in_specs=[spec, spec, spec, spec, dspec, spec, spec, rowspec, rowspec],
        out_specs=spec,
        out_shape=tile16,
        scratch_shapes=[pltpu.VMEM((lanes // 128, 2 * CHUNK, 2 * CHUNK), F32)],
        compiler_params=_params("parallel", "parallel", "arbitrary"),
        name="rwkv_scan_seq",
    )(p, m0, y1, nt, dec, bonus, g, lnx_w.reshape(1, d), lnx_b.reshape(1, d))


HALO = 32


CONV_STRIP = 512


def _conv_kernel(prev_ref, cur_ref, w_ref, b_ref, lnw_ref, lnb_ref, o_ref, buf_ref, acc_ref, sh_ref, *, width, tt, strip):
    first = pl.program_id(1) == 0
    buf_ref[0:HALO, :] = jnp.where(first, 0.0, prev_ref[0])
    buf_ref[HALO:HALO + tt, :] = cur_ref[0]
    off = HALO - (width - 1)
    for s in range(buf_ref.shape[1] // strip):
        lanes = slice(s * strip, (s + 1) * strip)
        acc = jnp.zeros((tt, strip), F32) + b_ref[:, lanes]
        for res in range(SUBLANES):
            span = tt + HALO - (SUBLANES if res else 0)
            if res:
                sh_ref[0:span, :] = buf_ref[res:res + span, lanes]
            for base in range(0, span - tt + 1, SUBLANES):
                j = base + res - off
                if 0 <= j < width:
                    src = sh_ref[base:base + tt, :] if res else buf_ref[base:base + tt, lanes]
                    acc = acc + src * w_ref[j:j + 1, lanes]
        acc_ref[:, lanes] = acc
    z = acc_ref[...]
    mu = jnp.mean(z, axis=-1, keepdims=True)
    zc = z - mu
    var = jnp.mean(zc * zc, axis=-1, keepdims=True)
    zn = zc * lax.rsqrt(var + LN_EPS) * lnw_ref[...] + lnb_ref[...]
    o_ref[0] = (zn * jax.nn.sigmoid(zn)).astype(o_ref.dtype)


def _dwconv_ln_swish(x, w, bias, ln_w, ln_b):
    bsz, tp, d = x.shape
    width = w.shape[0]
    assert width - 1 <= HALO
    tt = _divisor(tp, 320, HALO)
    ratio = tt // HALO
    wpad = jnp.pad(w, ((0, HALO - width), (0, 0)))
    row = pl.BlockSpec((1, d), lambda i, t: (0, 0))
    strip = _divisor(d, CONV_STRIP, 128)
    return pl.pallas_call(
        functools.partial(_conv_kernel, width=width, tt=tt, strip=strip),
        grid=(bsz, tp // tt),
        in_specs=[
            pl.BlockSpec((1, HALO, d), lambda i, t: (i, jnp.maximum(t * ratio - 1, 0), 0)),
            pl.BlockSpec((1, tt, d), lambda i, t: (i, t, 0)),
            pl.BlockSpec((HALO, d), lambda i, t: (0, 0)),
            row, row, row,
        ],
        out_specs=pl.BlockSpec((1, tt, d), lambda i, t: (i, t, 0)),
        out_shape=jax.ShapeDtypeStruct((bsz, tp, d), BF16),
        scratch_shapes=[pltpu.VMEM((HALO + tt, d), F32), pltpu.VMEM((tt, d), F32),
                        pltpu.VMEM((HALO + tt, strip), F32)],
        compiler_params=_params("parallel", "parallel"),
        name="dwconv_ln_swish",
    )(x, x, wpad, bias.reshape(1, d), ln_w.reshape(1, d), ln_b.reshape(1, d))


NEG = -jnp.inf
N_CAND = 16 + 7 * 8 + 8


def _cand_static():
    row = lax.broadcasted_iota(jnp.int32, (N_CAND, 1), 0)
    grp = jnp.where(row < 16, 0, (row - 16) // 8 + 1)
    bmid = (row - 16) % 8
    last = row >= 72
    a = jnp.where(row < 16, 0, jnp.where(last, row - 64, grp))
    b = jnp.where(row < 16, row, jnp.where(last, 0, bmid))
    valid = (a + 1) * (b + 1) <= PEER_TOPK
    return a * PEER_TOPK + b, valid


def _retrieve_kernel(q_ref, keys_ref, g_ref, i_ref, j_ref,
                     sv_ref, si_ref, cs_ref, cc_ref, ts_ref, tc_ref, *, nheads, nkeys, tt):
    kiota = lax.broadcasted_iota(jnp.int32, (nkeys, tt), 0).astype(F32)
    flat, valid = _cand_static()
    code0 = (flat * (nkeys * nkeys)).astype(F32)
    big = float(1 << 24)

    def head_body(h, carry):
        for half in range(2):
            hc = h * 2 + half
            q = q_ref[pl.ds(pl.multiple_of(hc * nkeys, nkeys), nkeys), :]
            s = jnp.dot(keys_ref[hc], q.astype(BF16), preferred_element_type=F32)
            for r in range(PEER_TOPK):
                m = jnp.max(s, axis=0, keepdims=True)
                idx = jnp.min(jnp.where(s == m, kiota, big), axis=0, keepdims=True)
                sv_ref[half, r:r + 1, :] = m
                si_ref[half, r:r + 1, :] = idx
                s = jnp.where(kiota == idx, NEG, s)
        cs_ref[0:16, :] = sv_ref[0, 0:1, :] + sv_ref[1, :, :]
        cc_ref[0:16, :] = si_ref[0, 0:1, :] * nkeys + si_ref[1, :, :]
        for a in range(1, 8):
            lo = 16 + (a - 1) * 8
            cs_ref[lo:lo + 8, :] = sv_ref[0, a:a + 1, :] + sv_ref[1, 0:8, :]
            cc_ref[lo:lo + 8, :] = si_ref[0, a:a + 1, :] * nkeys + si_ref[1, 0:8, :]
        cs_ref[72:80, :] = sv_ref[0, 8:16, :] + sv_ref[1, 0:1, :]
        cc_ref[72:80, :] = si_ref[0, 8:16, :] * nkeys + si_ref[1, 0:1, :]
        cand = jnp.where(valid, cs_ref[...], NEG)
        code = cc_ref[...] + code0
        for r in range(PEER_TOPK):
            m = jnp.max(cand, axis=0, keepdims=True)
            c = jnp.min(jnp.where(cand == m, code, big), axis=0, keepdims=True)
            ts_ref[r:r + 1, :] = m
            tc_ref[r:r + 1, :] = c
            cand = jnp.where(code == c, NEG, cand)
        base = pl.multiple_of(h * PEER_TOPK, PEER_TOPK)
        top = ts_ref[...]
        ex = jnp.exp(top - jnp.max(top, axis=0, keepdims=True))
        g_ref[pl.ds(base, PEER_TOPK), :] = ex / jnp.sum(ex, axis=0, keepdims=True)
        expert = tc_ref[...].astype(jnp.int32) % (nkeys * nkeys)
        i_ref[pl.ds(base, PEER_TOPK), :] = expert // nkeys
        j_ref[pl.ds(base, PEER_TOPK), :] = expert % nkeys
        return carry

    lax.fori_loop(0, nheads, head_body, 0)


def _peer_retrieve(qt, keys):
    ph, _, nkeys, dk = keys.shape
    m = qt.shape[1]
    assert nkeys == dk == 128
    tt = _divisor(m, 256, 128)
    keys2 = keys.reshape(ph * 2, nkeys, dk).astype(BF16)
    np_ = ph * PEER_TOPK
    out_spec = pl.BlockSpec((np_, tt), lambda i: (0, i))
    return pl.pallas_call(
        functools.partial(_retrieve_kernel, nheads=ph, nkeys=nkeys, tt=tt),
        grid=(m // tt,),
        in_specs=[
            pl.BlockSpec((ph * 2 * dk, tt), lambda i: (0, i)),
            pl.BlockSpec((ph * 2, nkeys, dk), lambda i: (0, 0, 0)),
        ],
        out_specs=[out_spec, out_spec, out_spec],
        out_shape=[
            jax.ShapeDtypeStruct((np_, m), F32),
            jax.ShapeDtypeStruct((np_, m), jnp.int32),
            jax.ShapeDtypeStruct((np_, m), jnp.int32),
        ],
        scratch_shapes=[
            pltpu.VMEM((2, PEER_TOPK, tt), F32),
            pltpu.VMEM((2, PEER_TOPK, tt), F32),
            pltpu.VMEM((N_CAND, tt), F32),
            pltpu.VMEM((N_CAND, tt), F32),
            pltpu.VMEM((PEER_TOPK, tt), F32),
            pltpu.VMEM((PEER_TOPK, tt), F32),
        ],
        compiler_params=_params("parallel"),
        name="peer_retrieve",
    )(qt, keys2)


DENSE_SUB = 512


def _gate_table_kernel(g_ref, i_ref, j_ref, o_ref, *, tb, nkeys):
    sub = lax.broadcasted_iota(jnp.int32, (nkeys, g_ref.shape[1]), 0)

    def body(t, carry):
        g = g_ref[pl.ds(t, 1), :]
        i = i_ref[pl.ds(t, 1), :]
        j = j_ref[pl.ds(t, 1), :]
        a_t = jnp.where(i == sub, g, 0.0).astype(BF16)
        b_t = jnp.where(j == sub, 1.0, 0.0).astype(BF16)
        w = lax.dot_general(a_t, b_t, (((1,), (1,)), ((), ())), preferred_element_type=F32)
        o_ref[:, pl.ds(t, 1)] = w.reshape(nkeys // SUBLANES, 1, SUBLANES, nkeys)
        return carry

    lax.fori_loop(0, tb, body, 0, unroll=32)


def _gate_table(g, i, j, nkeys):
    m, p = g.shape
    tb = _divisor(m, 64, 8)
    spec = pl.BlockSpec((tb, p), lambda t: (t, 0))
    ng = nkeys // SUBLANES
    w = pl.pallas_call(
        functools.partial(_gate_table_kernel, tb=tb, nkeys=nkeys),
        grid=(m // tb,),
        in_specs=[spec, spec, spec],
        out_specs=pl.BlockSpec((ng, tb, SUBLANES, nkeys), lambda t: (0, t, 0, 0)),
        out_shape=jax.ShapeDtypeStruct((ng, m, SUBLANES, nkeys), F32),
        compiler_params=_params("parallel"),
        name="peer_gate_table",
    )(g, i, j)
    return w.reshape(ng, m * SUBLANES, nkeys)


def _peer_dense_kernel(*refs, tm, nkeys, final_norm):
    if final_norm:
        h_ref, g_ref, u_ref, v_ref, w_ref, gf_ref, o_ref, x_ref = refs
    else:
        h_ref, g_ref, u_ref, v_ref, w_ref, o_ref, x_ref = refs
    k = pl.program_id(1)

    @pl.when(k == 0)
    def _():
        h = h_ref[...]
        o_ref[...] = h
        x_ref[...] = _rms(h, g_ref[...]).astype(BF16)

    per_sub = DENSE_SUB // nkeys
    for s in range(SUBLANES // per_sub):
        cols = slice(s * DENSE_SUB, (s + 1) * DENSE_SUB)
        act = lax.dot_general(x_ref[...], u_ref[cols, :], (((1,), (1,)), ((), ())),
                              preferred_element_type=F32)
        gelu = 0.5 * act * (1.0 + lax.erf(act * (1.0 / math.sqrt(2.0))))
        w = jnp.concatenate(
            [w_ref[pl.ds(s * per_sub + i, tm, stride=SUBLANES), :] for i in range(per_sub)], axis=1)
        o_ref[...] += jnp.dot((w * gelu).astype(BF16), v_ref[cols, :], preferred_element_type=F32)

    if final_norm:
        @pl.when(k == pl.num_programs(1) - 1)
        def _():
            o_ref[...] = _rms(o_ref[...], gf_ref[...])


def _peer_dense(h, norm_g, u, v, w, final_g=None):
    m, d = h.shape
    ng, _, nkeys = w.shape
    assert DENSE_SUB % nkeys == 0 and SUBLANES * nkeys % DENSE_SUB == 0
    te = SUBLANES * nkeys
    tm = _divisor(m, 832, 64)
    row = pl.BlockSpec((1, d), lambda i, k: (0, 0))
    args = [h, norm_g.reshape(1, d), u, v, w]
    specs = [
        pl.BlockSpec((tm, d), lambda i, k: (i, 0), pipeline_mode=pl.Buffered(1)),
        row,
        pl.BlockSpec((te, d), lambda i, k: (k, 0)),
        pl.BlockSpec((te, d), lambda i, k: (k, 0)),
        pl.BlockSpec((None, tm * SUBLANES, nkeys), lambda i, k: (k, i, 0)),
    ]
    if final_g is not None:
        args.append(final_g.reshape(1, d))
        specs.append(row)
    return pl.pallas_call(
        functools.partial(_peer_dense_kernel, tm=tm, nkeys=nkeys, final_norm=final_g is not None),
        grid=(m // tm, ng),
        in_specs=specs,
        out_specs=pl.BlockSpec((tm, d), lambda i, k: (i, 0)),
        out_shape=jax.ShapeDtypeStruct((m, d), F32),
        scratch_shapes=[pltpu.VMEM((tm, d), BF16)],
        compiler_params=_params("parallel", "arbitrary"),
        name="peer_dense",
    )(*args)


def _peer_layer(h, g_norm, wq, keys, u_tab, v_tab, final_g=None):
    nkeys = keys.shape[2]
    qt = _proj_t(wq.T.astype(BF16), h, g_norm)
    gate, ii, jj = _peer_retrieve(qt, keys)
    w = _gate_table(gate.T, ii.T, jj.T, nkeys)
    return _peer_dense(h, g_norm, u_tab.astype(BF16), v_tab.astype(BF16), w, final_g)


def _rwkv_layer(h, tp, g_norm, mix, w0, w1, w2, a0, a1, a2, g1, g2, k_k, k_a, r_k, w_rkv, w_o, lnx_w, lnx_b):
    m, d = h.shape
    bsz = m // tp
    nh = d // HEAD

    def shifted(n, weights, **kw):
        return _proj(h, weights, norm_g=g_norm, mix=mix[n], seq_len=tp, **kw)

    r = shifted(0, (w_rkv[0], None))
    k = shifted(2, (w_rkv[1], None))
    v = shifted(3, (w_rkv[2], None))
    lw = shifted(1, (w1, w2, w0), body="lora", mid="tanh", post="log_decay")
    a = shifted(4, (a1, a2, a0), body="lora", post="sigmoid")
    g = shifted(5, (g1, g2, None), body="lora", mid="sigmoid")

    def seq(z):
        return z.reshape(bsz, tp, d)

    out = _rwkv_mix(seq(r), seq(lw), seq(k), seq(v), seq(a), seq(g), k_k, k_a, r_k, lnx_w, lnx_b)
    return _proj(out.reshape(m, d), (w_o, None), residual=h)


def _conv_layer(h, tp, g_norm, pw1_w, pw1_b, dw_w, dw_b, ln_w, ln_b, pw2_w, pw2_b):
    m, d = h.shape
    z = _proj(h, (pw1_w, pw1_b), body="glu", norm_g=g_norm)
    z = _dwconv_ln_swish(z.reshape(m // tp, tp, d), dw_w, dw_b, ln_w, ln_b).reshape(m, d)
    return _proj(z, (pw2_w, pw2_b), residual=h)


def kernel(x, meta_tokens, norm_mix_a, rwkv_mix, rwkv_w0, rwkv_w1, rwkv_w2, rwkv_a0, rwkv_a1, rwkv_a2, rwkv_g1, rwkv_g2, rwkv_k_k, rwkv_k_a, rwkv_r_k, rwkv_w_rkv, rwkv_w_o, rwkv_lnx_w, rwkv_lnx_b, norm_mix_b, conv_pw1_w, conv_pw1_b, conv_dw_w, conv_dw_b, conv_ln_w, conv_ln_b, conv_pw2_w, conv_pw2_b, norm_ffn, peer_wq, peer_keys, peer_u, peer_v, norm_final):
    bsz, seq, d = x.shape
    n_meta = meta_tokens.shape[0]
    t = n_meta + seq
    tp = -(-t // 128) * 128 if t <= 128 else -(-t // CHUNK) * CHUNK
    depth = norm_ffn.shape[0]
    meta = jnp.broadcast_to(meta_tokens[None].astype(x.dtype), (bsz, n_meta, d))
    h = jnp.concatenate([meta, x, jnp.zeros((bsz, tp - t, d), x.dtype)], axis=1).reshape(bsz * tp, d)
    for i in range(depth):
        j = i // 2
        if i % 2 == 0:
            h = _rwkv_layer(
                h, tp, norm_mix_a[j], rwkv_mix[j], rwkv_w0[j], rwkv_w1[j], rwkv_w2[j], rwkv_a0[j], rwkv_a1[j],
                rwkv_a2[j], rwkv_g1[j], rwkv_g2[j], rwkv_k_k[j], rwkv_k_a[j], rwkv_r_k[j], rwkv_w_rkv[j],
                rwkv_w_o[j], rwkv_lnx_w[j], rwkv_lnx_b[j])
        else:
            h = _conv_layer(
                h, tp, norm_mix_b[j], conv_pw1_w[j], conv_pw1_b[j], conv_dw_w[j], conv_dw_b[j], conv_ln_w[j],
                conv_ln_b[j], conv_pw2_w[j], conv_pw2_b[j])
        h = _peer_layer(h, norm_ffn[i], peer_wq[i], peer_keys[i], peer_u[i], peer_v[i],
                        norm_final if i == depth - 1 else None)
    if depth == 0:
        h = _rms(h, norm_final)
    return h.reshape(bsz, tp, d)[:, n_meta:t]
```

```python
import functools
import math

import jax
import jax.numpy as jnp
from jax import lax
from jax.experimental import pallas as pl
from jax.experimental.pallas import tpu as pltpu

F32 = jnp.float32
BF16 = jnp.bfloat16

NORM_EPS = 1e-6
GN_EPS = 64e-5
KK_EPS = 1e-24
LN_EPS = 1e-5
HEAD = 64
CHUNK = 64
SUB = 16
PEER_TOPK = 16
SUBLANES = 8
VMEM_LIMIT = 56 * 1024 * 1024


def _divisor(n, target, mult):
    best = None
    for d in range(mult, min(n, target) + 1, mult):
        if n % d == 0:
            best = d
    assert best is not None, (n, target, mult)
    return best


def _params(*sem):
    return pltpu.CompilerParams(dimension_semantics=sem, vmem_limit_bytes=VMEM_LIMIT)


def _rms(x, g):
    return x * lax.rsqrt(jnp.mean(x * x, axis=-1, keepdims=True) + NORM_EPS) * g


def _proj_kernel(*refs, prologue, body, mid, post, residual, tm, seq_tiles):
    it = iter(refs)
    a_ref = next(it)
    prev_ref = next(it) if prologue == "shift" else None
    g_ref = next(it) if prologue != "none" else None
    mix_ref = next(it) if prologue == "shift" else None
    w_refs = [next(it) for _ in range({"mm": 2, "glu": 4, "lora": 3}[body])]
    res_ref = next(it) if residual else None
    o_ref = next(it)
    buf_ref = next(it) if prologue == "shift" else None

    if prologue == "none":
        a = a_ref[...]
    else:
        x = _rms(a_ref[...], g_ref[...])
        if prologue == "shift":
            first = (pl.program_id(1) % seq_tiles) == 0
            buf_ref[0:SUBLANES, :] = jnp.where(first, 0.0, _rms(prev_ref[...], g_ref[...]))
            buf_ref[SUBLANES:SUBLANES + tm, :] = x
            x = x + (buf_ref[SUBLANES - 1:SUBLANES - 1 + tm, :] - x) * mix_ref[...]
        a = x.astype(BF16)

    if body == "mm":
        w_ref, b_ref = w_refs
        z = jnp.dot(a, w_ref[...], preferred_element_type=F32) + b_ref[...]
    elif body == "glu":
        wl_ref, wr_ref, bl_ref, br_ref = w_refs
        z = ((jnp.dot(a, wl_ref[...], preferred_element_type=F32) + bl_ref[...])
             * jax.nn.sigmoid(jnp.dot(a, wr_ref[...], preferred_element_type=F32) + br_ref[...]))
    else:
        w1_ref, w2_ref, b_ref = w_refs
        hid = jnp.dot(a, w1_ref[...], preferred_element_type=F32)
        if mid == "tanh":
            hid = jnp.tanh(hid)
        elif mid == "sigmoid":
            hid = jax.nn.sigmoid(hid)
        z = jnp.dot(hid.astype(BF16), w2_ref[...], preferred_element_type=F32) + b_ref[...]
    if post == "log_decay":
        z = -jnp.exp(-jax.nn.softplus(-z) - 0.5)
    elif post == "sigmoid":
        z = jax.nn.sigmoid(z)
    if residual:
        z = z + res_ref[...]
    o_ref[...] = z.astype(o_ref.dtype)


def _proj(a, weights, *, body="mm", norm_g=None, mix=None, seq_len=None, mid=None, post=None,
          residual=None, out_dtype=F32):
    m, k = a.shape
    prologue = "none" if norm_g is None else ("norm" if mix is None else "shift")
    if body == "lora":
        w1, w2, bias = weights
        r = w1.shape[1]
        rp = -(-r // 128) * 128
        n = w2.shape[1]
        tn = n
        ws = [jnp.pad(w1, ((0, 0), (0, rp - r))).astype(BF16), jnp.pad(w2, ((0, rp - r), (0, 0))).astype(BF16)]
        w_specs = [pl.BlockSpec((k, rp), lambda j, i: (0, 0)), pl.BlockSpec((rp, n), lambda j, i: (0, 0))]
        biases = [bias]
    elif body == "glu":
        w, bias = weights
        n = w.shape[1] // 2
        tn = _divisor(n, 1024, 128)
        ws = [w[:, :n].astype(BF16), w[:, n:].astype(BF16)]
        w_specs = [pl.BlockSpec((k, tn), lambda j, i: (0, j))] * 2
        biases = [bias[:n], bias[n:]]
    else:
        w, bias = weights
        n = w.shape[1]
        tn = _divisor(n, 2048, 128)
        ws = [w.astype(BF16)]
        w_specs = [pl.BlockSpec((k, tn), lambda j, i: (0, j), pipeline_mode=pl.Buffered(1))]
        biases = [bias]
    biases = [(jnp.zeros((n,), F32) if b is None else b).reshape(1, n).astype(F32) for b in biases]
    b_specs = [pl.BlockSpec((1, tn), lambda j, i: (0, j))] * len(biases)

    rows = m if seq_len is None else seq_len
    tm = _divisor(rows, 448, 16)
    args = [a]
    specs = [pl.BlockSpec((tm, k), lambda j, i: (i, 0))]
    scratch = []
    if prologue == "shift":
        ratio = tm // SUBLANES
        args.append(a)
        specs.append(pl.BlockSpec((SUBLANES, k), lambda j, i: (jnp.maximum(i * ratio - 1, 0), 0)))
        scratch = [pltpu.VMEM((SUBLANES + tm, k), F32)]
    if prologue != "none":
        args.append(norm_g.reshape(1, k))
        specs.append(pl.BlockSpec((1, k), lambda j, i: (0, 0)))
    if prologue == "shift":
        args.append(mix.reshape(1, k))
        specs.append(pl.BlockSpec((1, k), lambda j, i: (0, 0)))
    args += ws + biases
    specs += w_specs + b_specs
    if residual is not None:
        args.append(residual)
        specs.append(pl.BlockSpec((tm, tn), lambda j, i: (i, j)))
    return pl.pallas_call(
        functools.partial(_proj_kernel, prologue=prologue, body=body, mid=mid, post=post,
                          residual=residual is not None, tm=tm, seq_tiles=rows // tm),
        grid=(n // tn, m // tm),
        in_specs=specs,
        out_specs=pl.BlockSpec((tm, tn), lambda j, i: (i, j)),
        out_shape=jax.ShapeDtypeStruct((m, n), out_dtype),
        scratch_shapes=scratch,
        compiler_params=_params("parallel", "parallel"),
        name="proj_" + prologue + "_" + body,
    )(*args)


def _lora3_kernel(a_ref, prev_ref, g_ref, mix_ref, *refs, tm, seq_tiles, mids, posts):
    n = len(mids)
    w_refs, o_refs, buf_ref = refs[:3 * n], refs[3 * n:4 * n], refs[4 * n]
    x = _rms(a_ref[...], g_ref[...])
    first = (pl.program_id(0) % seq_tiles) == 0
    buf_ref[0:SUBLANES, :] = jnp.where(first, 0.0, _rms(prev_ref[...], g_ref[...]))
    buf_ref[SUBLANES:SUBLANES + tm, :] = x
    dx = buf_ref[SUBLANES - 1:SUBLANES - 1 + tm, :] - x
    for i in range(n):
        w1_ref, w2_ref, b_ref = w_refs[3 * i:3 * i + 3]
        a = (x + dx * mix_ref[i:i + 1, :]).astype(BF16)
        hid = jnp.dot(a, w1_ref[...], preferred_element_type=F32)
        if mids[i] == "tanh":
            hid = jnp.tanh(hid)
        elif mids[i] == "sigmoid":
            hid = jax.nn.sigmoid(hid)
        z = jnp.dot(hid.astype(BF16), w2_ref[...], preferred_element_type=F32) + b_ref[...]
        if posts[i] == "log_decay":
            z = -jnp.exp(-jax.nn.softplus(-z) - 0.5)
        elif posts[i] == "sigmoid":
            z = jax.nn.sigmoid(z)
        o_refs[i][...] = z


def _lora3(h, norm_g, mixes, seq_len, loras, mids, posts):
    m, k = h.shape
    tm = _divisor(seq_len, 448, 16)
    ratio = tm // SUBLANES
    const = lambda i: (0, 0)
    args = [h, h, norm_g.reshape(1, k), mixes]
    specs = [pl.BlockSpec((tm, k), lambda i: (i, 0)),
             pl.BlockSpec((SUBLANES, k), lambda i: (jnp.maximum(i * ratio - 1, 0), 0)),
             pl.BlockSpec((1, k), const), pl.BlockSpec(mixes.shape, const)]
    n = None
    for w1, w2, bias in loras:
        r, n = w1.shape[1], w2.shape[1]
        rp = -(-r // 128) * 128
        args += [jnp.pad(w1, ((0, 0), (0, rp - r))).astype(BF16), jnp.pad(w2, ((0, rp - r), (0, 0))).astype(BF16),
                 (jnp.zeros((n,), F32) if bias is None else bias).reshape(1, n).astype(F32)]
        specs += [pl.BlockSpec((k, rp), const), pl.BlockSpec((rp, n), const), pl.BlockSpec((1, n), const)]
    return pl.pallas_call(
        functools.partial(_lora3_kernel, tm=tm, seq_tiles=seq_len // tm, mids=mids, posts=posts),
        grid=(m // tm,),
        in_specs=specs,
        out_specs=[pl.BlockSpec((tm, n), lambda i: (i, 0))] * len(loras),
        out_shape=[jax.ShapeDtypeStruct((m, n), F32)] * len(loras),
        scratch_shapes=[pltpu.VMEM((SUBLANES + tm, k), F32)],
        compiler_params=_params("parallel"),
        name="lora3",
    )(*args)


def _proj_t_kernel(w_ref, a_ref, g_ref, o_ref):
    a = _rms(a_ref[...], g_ref[...]).astype(BF16)
    o_ref[...] = lax.dot_general(w_ref[...], a, (((1,), (1,)), ((), ())), preferred_element_type=F32)


def _proj_t(wt, a, norm_g):
    n, k = wt.shape
    m = a.shape[0]
    tm = _divisor(m, 1024, 128)
    tn = _divisor(n, 1024, 128)
    return pl.pallas_call(
        _proj_t_kernel,
        grid=(n // tn, m // tm),
        in_specs=[
            pl.BlockSpec((tn, k), lambda j, i: (j, 0)),
            pl.BlockSpec((tm, k), lambda j, i: (i, 0)),
            pl.BlockSpec((1, k), lambda j, i: (0, 0)),
        ],
        out_specs=pl.BlockSpec((tn, tm), lambda j, i: (j, i)),
        out_shape=jax.ShapeDtypeStruct((n, m), F32),
        compiler_params=_params("parallel", "parallel"),
        name="proj_t",
    )(wt, a, norm_g.reshape(1, k))


def _bdot(a, b):
    return jnp.dot(a.astype(BF16), b.astype(BF16), preferred_element_type=F32)


def _bdot_nt(a, b):
    return lax.dot_general(a.astype(BF16), b.astype(BF16), (((1,), (1,)), ((), ())),
                           preferred_element_type=F32)


def _bdot_tn(a, b):
    return lax.dot_general(a.astype(BF16), b.astype(BF16), (((0,), (0,)), ((), ())),
                           preferred_element_type=F32)


def _big(x, lo_half):
    zero = jnp.zeros_like(x)
    return jnp.concatenate([jnp.where(lo_half, x, zero), jnp.where(lo_half, zero, x)], axis=0)


def _fold(x):
    c = x.shape[0] // 2
    return x[:c] + x[c:]


def _scan_prep_kernel(r_ref, lw_ref, k_ref, v_ref, a_ref, kkw_ref, kaw_ref, rkw_ref,
                      p_ref, m_ref, y1_ref, n_ref, dec_ref, bonus_ref, *, npairs):
    c = CHUNK
    c2 = 2 * c
    row = lax.broadcasted_iota(jnp.int32, (c2, c2), 0)
    col = lax.broadcasted_iota(jnp.int32, (c2, c2), 1)
    strict = (row % c) > (col % c)
    incl = (row % c) >= (col % c)
    same_blk = (row // SUB) == (col // SUB)
    eye = jnp.where(row == col, 1.0, 0.0).astype(F32)
    lo_half = lax.broadcasted_iota(jnp.int32, (c, c2), 1) < HEAD
    tri = lax.broadcasted_iota(jnp.int32, (c, c), 0) >= lax.broadcasted_iota(jnp.int32, (c, c), 1)
    ones_tril = jnp.where(tri, 1.0, 0.0).astype(F32)

    lw = lw_ref[0]
    g = jnp.dot(ones_tril, lw, preferred_element_type=F32, precision=lax.Precision.HIGHEST)
    gc = g[c - 1:c, :]
    eg = jnp.exp(g)
    eng = jnp.exp(-g)
    egc = jnp.exp(gc - g)
    r_all = r_ref[0]
    a_all = a_ref[0]
    v_all = v_ref[0]
    kkr = k_ref[0] * kkw_ref[...]
    k_all = k_ref[0] * (1.0 + (a_all - 1.0) * kaw_ref[...])
    kka = kkr * a_all
    kk_t = kkr * jnp.exp(g - lw)
    r_t = r_all * eg
    k_h = k_all * eng
    b_h = kka * eng
    k_g = k_all * egc
    b_g = kka * egc
    rk = r_all * k_all * rkw_ref[...]
    dec_ref[0, 0] = jnp.exp(gc)

    units = range(npairs)

    def pair(x, u):
        return _big(x[:, u * c2:(u + 1) * c2], lo_half)

    kk_rs = [lax.rsqrt(jnp.maximum(jnp.sum(jnp.square(pair(kkr, u)), axis=1, keepdims=True), KK_EPS))
             for u in units]
    kkt = [pair(kk_t, u) * kk_rs[u] for u in units]
    rt = [pair(r_t, u) for u in units]
    vb = [pair(v_all, u) for u in units]
    bg = [pair(b_g, u) * kk_rs[u] for u in units]
    for u in units:
        bonus_ref[0, :, u * c2:(u + 1) * c2] = _fold(jnp.sum(pair(rk, u), axis=1, keepdims=True) * vb[u])
    gm = [_bdot_nt(jnp.concatenate([kkt[u], rt[u]], axis=0),
                   jnp.concatenate([pair(b_h, u) * kk_rs[u], pair(k_h, u)], axis=0)) for u in units]
    a_b = [jnp.where(strict, gm[u][:c2, :c2], 0.0) for u in units]
    a_k = [jnp.where(strict, gm[u][:c2, c2:], 0.0) for u in units]
    a_rb = [jnp.where(incl, gm[u][c2:, :c2], 0.0) for u in units]
    a_rk = [jnp.where(incl, gm[u][c2:, c2:], 0.0) for u in units]
    x2 = [_bdot(jnp.concatenate([a_k[u], a_rk[u]], axis=0), vb[u]) for u in units]

    d = [jnp.where(same_blk, a_b[u], 0.0) for u in units]
    low = [a_b[u] - d[u] for u in units]
    p = [eye - d[u] for u in units]
    dk = [_bdot(d[u], d[u]) for u in units]
    levels = int(math.log2(SUB)) - 1
    for lvl in range(levels):
        if lvl < levels - 1:
            both = [_bdot(jnp.concatenate([p[u], dk[u]], axis=0), dk[u]) for u in units]
            p = [p[u] + both[u][:c2] for u in units]
            dk = [both[u][c2:] for u in units]
        else:
            p = [p[u] + _bdot(p[u], dk[u]) for u in units]
    e = [_bdot(p[u], low[u]) for u in units]
    q = [eye - e[u] for u in units]
    ek = e
    for _ in range(int(math.log2(CHUNK // SUB)) - 1):
        ek = [_bdot(ek[u], ek[u]) for u in units]
        q = [q[u] + _bdot(q[u], ek[u]) for u in units]
    t_inv = [_bdot(q[u], p[u]) for u in units]

    wu = [_bdot(t_inv[u], jnp.concatenate([kkt[u], x2[u][:c2]], axis=1)) for u in units]
    aw = [_bdot(a_rb[u], wu[u]) for u in units]
    m0 = [_bdot_tn(-bg[u], wu[u][:, :c2]) for u in units]
    nt = [_bdot_tn(jnp.concatenate([vb[u], wu[u][:, c2:]], axis=0),
                   jnp.concatenate([pair(k_g, u), -bg[u]], axis=0)) for u in units]
    for u in units:
        sl = slice(u * c2, (u + 1) * c2)
        p_ref[0, :, sl] = _fold(rt[u] - aw[u][:, :c2]).astype(p_ref.dtype)
        y1_ref[0, :, sl] = _fold(x2[u][c2:] - aw[u][:, c2:])
        m_ref[0, :, sl] = _fold(m0[u]).astype(m_ref.dtype)
        n_ref[0, :, sl] = _fold(nt[u])


def _scan_seq_kernel(p_ref, m_ref, y1_ref, n_ref, dec_ref, bonus_ref, g_ref, lnw_ref, lnb_ref,
                     o_ref, s_ref, *, npairs):
    c2 = 2 * CHUNK

    @pl.when(pl.program_id(2) == 0)
    def _():
        s_ref[...] = jnp.zeros_like(s_ref)

    lo_half = lax.broadcasted_iota(jnp.int32, (CHUNK, c2), 1) < HEAD
    own = ((lax.broadcasted_iota(jnp.int32, (c2, c2), 0) < CHUNK)
           == (lax.broadcasted_iota(jnp.int32, (c2, c2), 1) < HEAD))
    for u in range(npairs):
        sl = slice(u * c2, (u + 1) * c2)
        s = s_ref[u]
        y = _bdot_nt(_big(p_ref[0, :, sl], lo_half), s) + _big(y1_ref[0, :, sl], lo_half)
        s_ref[u] = (s * dec_ref[0, 0, :, sl] + _bdot_nt(s, _big(m_ref[0, :, sl], lo_half))
                    + _big(n_ref[0, :, sl], lo_half))
        mu = jnp.sum(y, axis=1, keepdims=True) * (1.0 / HEAD)
        yc = jnp.where(own, y - mu, 0.0)
        var = jnp.sum(yc * yc, axis=1, keepdims=True) * (1.0 / HEAD)
        yn = _fold(yc * lax.rsqrt(var + GN_EPS)) * lnw_ref[:, sl] + lnb_ref[:, sl]
        o_ref[0, :, sl] = ((yn + bonus_ref[0, :, sl]) * g_ref[0, :, sl]).astype(o_ref.dtype)


def _rwkv_mix(r, lw, k, v, a, g, k_k, k_a, r_k, lnx_w, lnx_b):
    bsz, tp, d = r.shape
    nc = tp // CHUNK
    lanes = _divisor(d, 1024, 128)
    spec = pl.BlockSpec((1, CHUNK, lanes), lambda i, j, c: (i, c, j))
    dspec = pl.BlockSpec((1, 1, 1, lanes), lambda i, j, c: (i, c, 0, j))
    rowspec = pl.BlockSpec((1, lanes), lambda i, j, c: (0, j))
    tile = jax.ShapeDtypeStruct((bsz, tp, d), F32)
    tile16 = jax.ShapeDtypeStruct((bsz, tp, d), BF16)
    p, m0, y1, nt, dec, bonus = pl.pallas_call(
        functools.partial(_scan_prep_kernel, npairs=lanes // 128),
        grid=(bsz, d // lanes, nc),
        in_specs=[spec] * 5 + [rowspec] * 3,
        out_specs=[spec, spec, spec, spec, dspec, spec],
        out_shape=[tile16, tile16, tile, tile, jax.ShapeDtypeStruct((bsz, nc, 1, d), F32), tile],
        compiler_params=_params("parallel", "parallel", "parallel"),
        name="rwkv_scan_prep",
    )(r, lw, k, v, a, k_k.reshape(1, d), k_a.reshape(1, d), r_k.reshape(1, d))
    lanes = _divisor(d, 2048, 128)
    spec = pl.BlockSpec((1, CHUNK, lanes), lambda i, j, c: (i, c, j))
    dspec = pl.BlockSpec((1, 1, 1, lanes), lambda i, j, c: (i, c, 0, j))
    rowspec = pl.BlockSpec((1, lanes), lambda i, j, c: (0, j))
    return pl.pallas_call(
        functools.partial(_scan_seq_kernel, npairs=lanes // 128),
        grid=(bsz, d // lanes, nc),
        in_specs=[spec, spec, spec, spec, dspec, spec, spec, rowspec, rowspec],
        out_specs=spec,
        out_shape=tile16,
        scratch_shapes=[pltpu.VMEM((lanes // 128, 2 * CHUNK, 2 * CHUNK), F32)],
        compiler_params=_params("parallel", "parallel", "arbitrary"),
        name="rwkv_scan_seq",
    )(p, m0, y1, nt, dec, bonus, g, lnx_w.reshape(1, d), lnx_b.reshape(1, d))


HALO = 32


CONV_STRIP = 512


def _conv_kernel(prev_ref, cur_ref, w_ref, b_ref, lnw_ref, lnb_ref, o_ref, buf_ref, acc_ref, sh_ref, *, width, tt, strip):
    first = pl.program_id(1) == 0
    buf_ref[0:HALO, :] = jnp.where(first, 0.0, prev_ref[0])
    buf_ref[HALO:HALO + tt, :] = cur_ref[0]
    off = HALO - (width - 1)
    for s in range(buf_ref.shape[1] // strip):
        lanes = slice(s * strip, (s + 1) * strip)
        acc = jnp.zeros((tt, strip), F32) + b_ref[:, lanes]
        for res in range(SUBLANES):
            span = tt + HALO - (SUBLANES if res else 0)
            if res:
                sh_ref[0:span, :] = buf_ref[res:res + span, lanes]
            for base in range(0, span - tt + 1, SUBLANES):
                j = base + res - off
                if 0 <= j < width:
                    src = sh_ref[base:base + tt, :] if res else buf_ref[base:base + tt, lanes]
                    acc = acc + src * w_ref[j:j + 1, lanes]
        acc_ref[:, lanes] = acc
    z = acc_ref[...]
    mu = jnp.mean(z, axis=-1, keepdims=True)
    zc = z - mu
    var = jnp.mean(zc * zc, axis=-1, keepdims=True)
    zn = zc * lax.rsqrt(var + LN_EPS) * lnw_ref[...] + lnb_ref[...]
    o_ref[0] = (zn * jax.nn.sigmoid(zn)).astype(o_ref.dtype)


def _dwconv_ln_swish(x, w, bias, ln_w, ln_b):
    bsz, tp, d = x.shape
    width = w.shape[0]
    assert width - 1 <= HALO
    tt = _divisor(tp, 320, HALO)
    ratio = tt // HALO
    wpad = jnp.pad(w, ((0, HALO - width), (0, 0)))
    row = pl.BlockSpec((1, d), lambda i, t: (0, 0))
    strip = _divisor(d, CONV_STRIP, 128)
    return pl.pallas_call(
        functools.partial(_conv_kernel, width=width, tt=tt, strip=strip),
        grid=(bsz, tp // tt),
        in_specs=[
            pl.BlockSpec((1, HALO, d), lambda i, t: (i, jnp.maximum(t * ratio - 1, 0), 0)),
            pl.BlockSpec((1, tt, d), lambda i, t: (i, t, 0)),
            pl.BlockSpec((HALO, d), lambda i, t: (0, 0)),
            row, row, row,
        ],
        out_specs=pl.BlockSpec((1, tt, d), lambda i, t: (i, t, 0)),
        out_shape=jax.ShapeDtypeStruct((bsz, tp, d), BF16),
        scratch_shapes=[pltpu.VMEM((HALO + tt, d), F32), pltpu.VMEM((tt, d), F32),
                        pltpu.VMEM((HALO + tt, strip), F32)],
        compiler_params=_params("parallel", "parallel"),
        name="dwconv_ln_swish",
    )(x, x, wpad, bias.reshape(1, d), ln_w.reshape(1, d), ln_b.reshape(1, d))


NEG = -jnp.inf
N_CAND = 16 + 7 * 8 + 8


def _cand_static():
    row = lax.broadcasted_iota(jnp.int32, (N_CAND, 1), 0)
    grp = jnp.where(row < 16, 0, (row - 16) // 8 + 1)
    bmid = (row - 16) % 8
    last = row >= 72
    a = jnp.where(row < 16, 0, jnp.where(last, row - 64, grp))
    b = jnp.where(row < 16, row, jnp.where(last, 0, bmid))
    valid = (a + 1) * (b + 1) <= PEER_TOPK
    return a * PEER_TOPK + b, valid


def _retrieve_kernel(q_ref, keys_ref, g_ref, i_ref, j_ref,
                     sv_ref, si_ref, cs_ref, cc_ref, ts_ref, tc_ref, kiota_ref, *, nheads, nkeys, tt):
    kiota_ref[...] = lax.broadcasted_iota(jnp.int32, (nkeys, tt), 0).astype(F32)
    flat, valid = _cand_static()
    code0 = (flat * (nkeys * nkeys)).astype(F32)
    big = float(1 << 24)

    def sub_keys(h):
        slot = h % 2
        s = []
        for half in range(2):
            hc = h * 2 + half
            q = q_ref[hc * nkeys:(hc + 1) * nkeys, :]
            s.append(jnp.dot(keys_ref[hc], q.astype(BF16), preferred_element_type=F32))
        for r in range(PEER_TOPK):
            for half in range(2):
                m = jnp.max(s[half], axis=0, keepdims=True)
                idx = jnp.min(jnp.where(s[half] == m, kiota_ref[...], big), axis=0, keepdims=True)
                sv_ref[slot, half, r:r + 1, :] = m
                si_ref[slot, half, r:r + 1, :] = idx
                s[half] = jnp.where(kiota_ref[...] == idx, NEG, s[half])

    def experts(h):
        sv = sv_ref.at[h % 2]
        si = si_ref.at[h % 2]
        cs_ref[0:16, :] = sv[0, 0:1, :] + sv[1, :, :]
        cc_ref[0:16, :] = si[0, 0:1, :] * nkeys + si[1, :, :]
        for a in range(1, 8):
            lo = 16 + (a - 1) * 8
            cs_ref[lo:lo + 8, :] = sv[0, a:a + 1, :] + sv[1, 0:8, :]
            cc_ref[lo:lo + 8, :] = si[0, a:a + 1, :] * nkeys + si[1, 0:8, :]
        cs_ref[72:80, :] = sv[0, 8:16, :] + sv[1, 0:1, :]
        cc_ref[72:80, :] = si[0, 8:16, :] * nkeys + si[1, 0:1, :]
        cand = jnp.where(valid, cs_ref[...], NEG)
        cc_ref[...] = cc_ref[...] + code0
        for r in range(PEER_TOPK):
            m = jnp.max(cand, axis=0, keepdims=True)
            c = jnp.min(jnp.where(cand == m, cc_ref[...], big), axis=0, keepdims=True)
            ts_ref[r:r + 1, :] = m
            tc_ref[r:r + 1, :] = c
            cand = jnp.where(cc_ref[...] == c, NEG, cand)
        rows = slice(h * PEER_TOPK, (h + 1) * PEER_TOPK)
        top = ts_ref[...]
        ex = jnp.exp(top - jnp.max(top, axis=0, keepdims=True))
        g_ref[rows, :] = ex / jnp.sum(ex, axis=0, keepdims=True)
        expert = tc_ref[...].astype(jnp.int32) % (nkeys * nkeys)
        i_ref[rows, :] = expert // nkeys
        j_ref[rows, :] = expert % nkeys

    sub_keys(0)
    for h in range(nheads):
        if h + 1 < nheads:
            sub_keys(h + 1)
        experts(h)


def _peer_retrieve(qt, keys):
    ph, _, nkeys, dk = keys.shape
    m = qt.shape[1]
    assert nkeys == dk == 128
    tt = 128
    keys2 = keys.reshape(ph * 2, nkeys, dk).astype(BF16)
    np_ = ph * PEER_TOPK
    out_spec = pl.BlockSpec((np_, tt), lambda i: (0, i))
    return pl.pallas_call(
        functools.partial(_retrieve_kernel, nheads=ph, nkeys=nkeys, tt=tt),
        grid=(m // tt,),
        in_specs=[
            pl.BlockSpec((ph * 2 * dk, tt), lambda i: (0, i)),
            pl.BlockSpec((ph * 2, nkeys, dk), lambda i: (0, 0, 0)),
        ],
        out_specs=[out_spec, out_spec, out_spec],
        out_shape=[
            jax.ShapeDtypeStruct((np_, m), F32),
            jax.ShapeDtypeStruct((np_, m), jnp.int32),
            jax.ShapeDtypeStruct((np_, m), jnp.int32),
        ],
        scratch_shapes=[
            pltpu.VMEM((2, 2, PEER_TOPK, tt), F32),
            pltpu.VMEM((2, 2, PEER_TOPK, tt), F32),
            pltpu.VMEM((N_CAND, tt), F32),
            pltpu.VMEM((N_CAND, tt), F32),
            pltpu.VMEM((PEER_TOPK, tt), F32),
            pltpu.VMEM((PEER_TOPK, tt), F32),
            pltpu.VMEM((nkeys, tt), F32),
        ],
        compiler_params=_params("parallel"),
        name="peer_retrieve",
    )(qt, keys2)


DENSE_SUB = 512


def _gate_table_kernel(g_ref, i_ref, j_ref, o_ref, *, tb, nkeys):
    sub = lax.broadcasted_iota(jnp.int32, (nkeys, g_ref.shape[1]), 0)

    def body(t, carry):
        g = g_ref[pl.ds(t, 1), :]
        i = i_ref[pl.ds(t, 1), :]
        j = j_ref[pl.ds(t, 1), :]
        a_t = jnp.where(i == sub, g, 0.0).astype(BF16)
        b_t = jnp.where(j == sub, 1.0, 0.0).astype(BF16)
        w = lax.dot_general(a_t, b_t, (((1,), (1,)), ((), ())), preferred_element_type=F32)
        o_ref[:, pl.ds(t, 1)] = w.reshape(nkeys // SUBLANES, 1, SUBLANES, nkeys)
        return carry

    lax.fori_loop(0, tb, body, 0, unroll=32)


def _gate_table(g, i, j, nkeys):
    m, p = g.shape
    tb = _divisor(m, 64, 8)
    spec = pl.BlockSpec((tb, p), lambda t: (t, 0))
    ng = nkeys // SUBLANES
    w = pl.pallas_call(
        functools.partial(_gate_table_kernel, tb=tb, nkeys=nkeys),
        grid=(m // tb,),
        in_specs=[spec, spec, spec],
        out_specs=pl.BlockSpec((ng, tb, SUBLANES, nkeys), lambda t: (0, t, 0, 0)),
        out_shape=jax.ShapeDtypeStruct((ng, m, SUBLANES, nkeys), F32),
        compiler_params=_params("parallel"),
        name="peer_gate_table",
    )(g, i, j)
    return w.reshape(ng, m * SUBLANES, nkeys)


def _peer_dense_kernel(*refs, tm, nkeys, final_norm):
    if final_norm:
        h_ref, g_ref, u_ref, v_ref, w_ref, gf_ref, o_ref, x_ref = refs
    else:
        h_ref, g_ref, u_ref, v_ref, w_ref, o_ref, x_ref = refs
    k = pl.program_id(1)

    @pl.when(k == 0)
    def _():
        h = h_ref[...]
        o_ref[...] = h
        x_ref[...] = _rms(h, g_ref[...]).astype(BF16)

    per_sub = DENSE_SUB // nkeys
    for s in range(SUBLANES // per_sub):
        cols = slice(s * DENSE_SUB, (s + 1) * DENSE_SUB)
        act = lax.dot_general(x_ref[...], u_ref[cols, :], (((1,), (1,)), ((), ())),
                              preferred_element_type=F32)
        gelu = 0.5 * act * (1.0 + lax.erf(act * (1.0 / math.sqrt(2.0))))
        w = jnp.concatenate(
            [w_ref[pl.ds(s * per_sub + i, tm, stride=SUBLANES), :] for i in range(per_sub)], axis=1)
        o_ref[...] += jnp.dot((w * gelu).astype(BF16), v_ref[cols, :], preferred_element_type=F32)

    if final_norm:
        @pl.when(k == pl.num_programs(1) - 1)
        def _():
            o_ref[...] = _rms(o_ref[...], gf_ref[...])


def _peer_dense(h, norm_g, u, v, w, final_g=None):
    m, d = h.shape
    ng, _, nkeys = w.shape
    assert DENSE_SUB % nkeys == 0 and SUBLANES * nkeys % DENSE_SUB == 0
    te = SUBLANES * nkeys
    tm = _divisor(m, 832, 64)
    row = pl.BlockSpec((1, d), lambda i, k: (0, 0))
    args = [h, norm_g.reshape(1, d), u, v, w]
    specs = [
        pl.BlockSpec((tm, d), lambda i, k: (i, 0), pipeline_mode=pl.Buffered(1)),
        row,
        pl.BlockSpec((te, d), lambda i, k: (k, 0)),
        pl.BlockSpec((te, d), lambda i, k: (k, 0)),
        pl.BlockSpec((None, tm * SUBLANES, nkeys), lambda i, k: (k, i, 0)),
    ]
    if final_g is not None:
        args.append(final_g.reshape(1, d))
        specs.append(row)
    return pl.pallas_call(
        functools.partial(_peer_dense_kernel, tm=tm, nkeys=nkeys, final_norm=final_g is not None),
        grid=(m // tm, ng),
        in_specs=specs,
        out_specs=pl.BlockSpec((tm, d), lambda i, k: (i, 0)),
        out_shape=jax.ShapeDtypeStruct((m, d), F32),
        scratch_shapes=[pltpu.VMEM((tm, d), BF16)],
        compiler_params=_params("parallel", "arbitrary"),
        name="peer_dense",
    )(*args)


def _peer_layer(h, g_norm, wq, keys, u_tab, v_tab, final_g=None):
    nkeys = keys.shape[2]
    qt = _proj_t(wq.T.astype(BF16), h, g_norm)
    gate, ii, jj = _peer_retrieve(qt, keys)
    w = _gate_table(gate.T, ii.T, jj.T, nkeys)
    return _peer_dense(h, g_norm, u_tab.astype(BF16), v_tab.astype(BF16), w, final_g)


def _rwkv_layer(h, tp, g_norm, mix, w0, w1, w2, a0, a1, a2, g1, g2, k_k, k_a, r_k, w_rkv, w_o, lnx_w, lnx_b):
    m, d = h.shape
    bsz = m // tp
    nh = d // HEAD

    def shifted(n, weights, **kw):
        return _proj(h, weights, norm_g=g_norm, mix=mix[n], seq_len=tp, **kw)

    r = shifted(0, (w_rkv[0], None))
    k = shifted(2, (w_rkv[1], None))
    v = shifted(3, (w_rkv[2], None))
    lw, a, g = _lora3(h, g_norm, jnp.stack([mix[1], mix[4], mix[5]]), tp,
                      [(w1, w2, w0), (a1, a2, a0), (g1, g2, None)],
                      mids=("tanh", None, "sigmoid"), posts=("log_decay", "sigmoid", None))

    def seq(z):
        return z.reshape(bsz, tp, d)

    out = _rwkv_mix(seq(r), seq(lw), seq(k), seq(v), seq(a), seq(g), k_k, k_a, r_k, lnx_w, lnx_b)
    return _proj(out.reshape(m, d), (w_o, None), residual=h)


def _conv_layer(h, tp, g_norm, pw1_w, pw1_b, dw_w, dw_b, ln_w, ln_b, pw2_w, pw2_b):
    m, d = h.shape
    z = _proj(h, (pw1_w, pw1_b), body="glu", norm_g=g_norm)
    z = _dwconv_ln_swish(z.reshape(m // tp, tp, d), dw_w, dw_b, ln_w, ln_b).reshape(m, d)
    return _proj(z, (pw2_w, pw2_b), residual=h)


def kernel(x, meta_tokens, norm_mix_a, rwkv_mix, rwkv_w0, rwkv_w1, rwkv_w2, rwkv_a0, rwkv_a1, rwkv_a2, rwkv_g1, rwkv_g2, rwkv_k_k, rwkv_k_a, rwkv_r_k, rwkv_w_rkv, rwkv_w_o, rwkv_lnx_w, rwkv_lnx_b, norm_mix_b, conv_pw1_w, conv_pw1_b, conv_dw_w, conv_dw_b, conv_ln_w, conv_ln_b, conv_pw2_w, conv_pw2_b, norm_ffn, peer_wq, peer_keys, peer_u, peer_v, norm_final):
    bsz, seq, d = x.shape
    n_meta = meta_tokens.shape[0]
    t = n_meta + seq
    tp = -(-t // 128) * 128 if t <= 128 else -(-t // CHUNK) * CHUNK
    depth = norm_ffn.shape[0]
    meta = jnp.broadcast_to(meta_tokens[None].astype(x.dtype), (bsz, n_meta, d))
    h = jnp.concatenate([meta, x, jnp.zeros((bsz, tp - t, d), x.dtype)], axis=1).reshape(bsz * tp, d)
    for i in range(depth):
        j = i // 2
        if i % 2 == 0:
            h = _rwkv_layer(
                h, tp, norm_mix_a[j], rwkv_mix[j], rwkv_w0[j], rwkv_w1[j], rwkv_w2[j], rwkv_a0[j], rwkv_a1[j],
                rwkv_a2[j], rwkv_g1[j], rwkv_g2[j], rwkv_k_k[j], rwkv_k_a[j], rwkv_r_k[j], rwkv_w_rkv[j],
                rwkv_w_o[j], rwkv_lnx_w[j], rwkv_lnx_b[j])
        else:
            h = _conv_layer(
                h, tp, norm_mix_b[j], conv_pw1_w[j], conv_pw1_b[j], conv_dw_w[j], conv_dw_b[j], conv_ln_w[j],
                conv_ln_b[j], conv_pw2_w[j], conv_pw2_b[j])
        h = _peer_layer(h, norm_ffn[i], peer_wq[i], peer_keys[i], peer_u[i], peer_v[i],
                        norm_final if i == depth - 1 else None)
    if depth == 0:
        h = _rms(h, norm_final)
    return h.reshape(bsz, tp, d)[:, n_meta:t]
```

```python
import functools
import math

import jax
import jax.numpy as jnp
from jax import lax
from jax.experimental import pallas as pl
from jax.experimental.pallas import tpu as pltpu

F32 = jnp.float32
BF16 = jnp.bfloat16

NORM_EPS = 1e-6
GN_EPS = 64e-5
KK_EPS = 1e-24
LN_EPS = 1e-5
HEAD = 64
CHUNK = 64
SUB = 16
PEER_TOPK = 16
SUBLANES = 8
VMEM_LIMIT = 56 * 1024 * 1024


def _divisor(n, target, mult):
    best = None
    for d in range(mult, min(n, target) + 1, mult):
        if n % d == 0:
            best = d
    assert best is not None, (n, target, mult)
    return best


def _params(*sem):
    return pltpu.CompilerParams(dimension_semantics=sem, vmem_limit_bytes=VMEM_LIMIT)


def _rms(x, g):
    return x * lax.rsqrt(jnp.mean(x * x, axis=-1, keepdims=True) + NORM_EPS) * g


def _proj_kernel(*refs, prologue, body, mid, post, residual, tm, seq_tiles):
    it = iter(refs)
    a_ref = next(it)
    prev_ref = next(it) if prologue == "shift" else None
    g_ref = next(it) if prologue != "none" else None
    mix_ref = next(it) if prologue == "shift" else None
    w_refs = [next(it) for _ in range({"mm": 2, "glu": 4, "lora": 3}[body])]
    res_ref = next(it) if residual else None
    o_ref = next(it)
    buf_ref = next(it) if prologue == "shift" else None

    if prologue == "none":
        a = a_ref[...]
    else:
        x = _rms(a_ref[...], g_ref[...])
        if prologue == "shift":
            first = (pl.program_id(1) % seq_tiles) == 0
            buf_ref[0:SUBLANES, :] = jnp.where(first, 0.0, _rms(prev_ref[...], g_ref[...]))
            buf_ref[SUBLANES:SUBLANES + tm, :] = x
            x = x + (buf_ref[SUBLANES - 1:SUBLANES - 1 + tm, :] - x) * mix_ref[...]
        a = x.astype(BF16)

    if body == "mm":
        w_ref, b_ref = w_refs
        z = jnp.dot(a, w_ref[...], preferred_element_type=F32) + b_ref[...]
    elif body == "glu":
        wl_ref, wr_ref, bl_ref, br_ref = w_refs
        z = ((jnp.dot(a, wl_ref[...], preferred_element_type=F32) + bl_ref[...])
             * jax.nn.sigmoid(jnp.dot(a, wr_ref[...], preferred_element_type=F32) + br_ref[...]))
    else:
        w1_ref, w2_ref, b_ref = w_refs
        hid = jnp.dot(a, w1_ref[...], preferred_element_type=F32)
        if mid == "tanh":
            hid = jnp.tanh(hid)
        elif mid == "sigmoid":
            hid = jax.nn.sigmoid(hid)
        z = jnp.dot(hid.astype(BF16), w2_ref[...], preferred_element_type=F32) + b_ref[...]
    if post == "log_decay":
        z = -jnp.exp(-jax.nn.softplus(-z) - 0.5)
    elif post == "sigmoid":
        z = jax.nn.sigmoid(z)
    if residual:
        z = z + res_ref[...]
    o_ref[...] = z.astype(o_ref.dtype)


def _proj(a, weights, *, body="mm", norm_g=None, mix=None, seq_len=None, mid=None, post=None,
          residual=None, out_dtype=F32):
    m, k = a.shape
    prologue = "none" if norm_g is None else ("norm" if mix is None else "shift")
    if body == "lora":
        w1, w2, bias = weights
        r = w1.shape[1]
        rp = -(-r // 128) * 128
        n = w2.shape[1]
        tn = n
        ws = [jnp.pad(w1, ((0, 0), (0, rp - r))).astype(BF16), jnp.pad(w2, ((0, rp - r), (0, 0))).astype(BF16)]
        w_specs = [pl.BlockSpec((k, rp), lambda j, i: (0, 0)), pl.BlockSpec((rp, n), lambda j, i: (0, 0))]
        biases = [bias]
    elif body == "glu":
        w, bias = weights
        n = w.shape[1] // 2
        tn = _divisor(n, 1024, 128)
        ws = [w[:, :n].astype(BF16), w[:, n:].astype(BF16)]
        w_specs = [pl.BlockSpec((k, tn), lambda j, i: (0, j))] * 2
        biases = [bias[:n], bias[n:]]
    else:
        w, bias = weights
        n = w.shape[1]
        tn = _divisor(n, 2048, 128)
        ws = [w.astype(BF16)]
        w_specs = [pl.BlockSpec((k, tn), lambda j, i: (0, j), pipeline_mode=pl.Buffered(1))]
        biases = [bias]
    biases = [(jnp.zeros((n,), F32) if b is None else b).reshape(1, n).astype(F32) for b in biases]
    b_specs = [pl.BlockSpec((1, tn), lambda j, i: (0, j))] * len(biases)

    rows = m if seq_len is None else seq_len
    tm = _divisor(rows, 448, 16)
    args = [a]
    specs = [pl.BlockSpec((tm, k), lambda j, i: (i, 0))]
    scratch = []
    if prologue == "shift":
        ratio = tm // SUBLANES
        args.append(a)
        specs.append(pl.BlockSpec((SUBLANES, k), lambda j, i: (jnp.maximum(i * ratio - 1, 0), 0)))
        scratch = [pltpu.VMEM((SUBLANES + tm, k), F32)]
    if prologue != "none":
        args.append(norm_g.reshape(1, k))
        specs.append(pl.BlockSpec((1, k), lambda j, i: (0, 0)))
    if prologue == "shift":
        args.append(mix.reshape(1, k))
        specs.append(pl.BlockSpec((1, k), lambda j, i: (0, 0)))
    args += ws + biases
    specs += w_specs + b_specs
    if residual is not None:
        args.append(residual)
        specs.append(pl.BlockSpec((tm, tn), lambda j, i: (i, j)))
    return pl.pallas_call(
        functools.partial(_proj_kernel, prologue=prologue, body=body, mid=mid, post=post,
                          residual=residual is not None, tm=tm, seq_tiles=rows // tm),
        grid=(n // tn, m // tm),
        in_specs=specs,
        out_specs=pl.BlockSpec((tm, tn), lambda j, i: (i, j)),
        out_shape=jax.ShapeDtypeStruct((m, n), out_dtype),
        scratch_shapes=scratch,
        compiler_params=_params("parallel", "parallel"),
        name="proj_" + prologue + "_" + body,
    )(*args)


def _lora3_kernel(a_ref, prev_ref, g_ref, mix_ref, *refs, tm, seq_tiles, mids, posts):
    n = len(mids)
    w_refs, o_refs, buf_ref = refs[:3 * n], refs[3 * n:4 * n], refs[4 * n]
    x = _rms(a_ref[...], g_ref[...])
    first = (pl.program_id(0) % seq_tiles) == 0
    buf_ref[0:SUBLANES, :] = jnp.where(first, 0.0, _rms(prev_ref[...], g_ref[...]))
    buf_ref[SUBLANES:SUBLANES + tm, :] = x
    dx = buf_ref[SUBLANES - 1:SUBLANES - 1 + tm, :] - x
    for i in range(n):
        w1_ref, w2_ref, b_ref = w_refs[3 * i:3 * i + 3]
        a = (x + dx * mix_ref[i:i + 1, :]).astype(BF16)
        hid = jnp.dot(a, w1_ref[...], preferred_element_type=F32)
        if mids[i] == "tanh":
            hid = jnp.tanh(hid)
        elif mids[i] == "sigmoid":
            hid = jax.nn.sigmoid(hid)
        z = jnp.dot(hid.astype(BF16), w2_ref[...], preferred_element_type=F32) + b_ref[...]
        if posts[i] == "log_decay":
            z = -jnp.exp(-jax.nn.softplus(-z) - 0.5)
        elif posts[i] == "sigmoid":
            z = jax.nn.sigmoid(z)
        o_refs[i][...] = z


def _lora3(h, norm_g, mixes, seq_len, loras, mids, posts):
    m, k = h.shape
    tm = _divisor(seq_len, 448, 16)
    ratio = tm // SUBLANES
    const = lambda i: (0, 0)
    args = [h, h, norm_g.reshape(1, k), mixes]
    specs = [pl.BlockSpec((tm, k), lambda i: (i, 0)),
             pl.BlockSpec((SUBLANES, k), lambda i: (jnp.maximum(i * ratio - 1, 0), 0)),
             pl.BlockSpec((1, k), const), pl.BlockSpec(mixes.shape, const)]
    n = None
    for w1, w2, bias in loras:
        r, n = w1.shape[1], w2.shape[1]
        rp = -(-r // 128) * 128
        args += [jnp.pad(w1, ((0, 0), (0, rp - r))).astype(BF16), jnp.pad(w2, ((0, rp - r), (0, 0))).astype(BF16),
                 (jnp.zeros((n,), F32) if bias is None else bias).reshape(1, n).astype(F32)]
        specs += [pl.BlockSpec((k, rp), const), pl.BlockSpec((rp, n), const), pl.BlockSpec((1, n), const)]
    return pl.pallas_call(
        functools.partial(_lora3_kernel, tm=tm, seq_tiles=seq_len // tm, mids=mids, posts=posts),
        grid=(m // tm,),
        in_specs=specs,
        out_specs=[pl.BlockSpec((tm, n), lambda i: (i, 0))] * len(loras),
        out_shape=[jax.ShapeDtypeStruct((m, n), F32)] * len(loras),
        scratch_shapes=[pltpu.VMEM((SUBLANES + tm, k), F32)],
        compiler_params=_params("parallel"),
        name="lora3",
    )(*args)


def _proj_t_kernel(w_ref, a_ref, g_ref, o_ref):
    a = _rms(a_ref[...], g_ref[...]).astype(BF16)
    o_ref[...] = lax.dot_general(w_ref[...], a, (((1,), (1,)), ((), ())), preferred_element_type=F32)


def _proj_t(wt, a, norm_g):
    n, k = wt.shape
    m = a.shape[0]
    tm = _divisor(m, 1024, 128)
    tn = _divisor(n, 2048, 128)
    return pl.pallas_call(
        _proj_t_kernel,
        grid=(n // tn, m // tm),
        in_specs=[
            pl.BlockSpec((tn, k), lambda j, i: (j, 0), pipeline_mode=pl.Buffered(1)),
            pl.BlockSpec((tm, k), lambda j, i: (i, 0)),
            pl.BlockSpec((1, k), lambda j, i: (0, 0)),
        ],
        out_specs=pl.BlockSpec((tn, tm), lambda j, i: (j, i)),
        out_shape=jax.ShapeDtypeStruct((n, m), F32),
        compiler_params=_params("parallel", "parallel"),
        name="proj_t",
    )(wt, a, norm_g.reshape(1, k))


def _bdot(a, b):
    return jnp.dot(a.astype(BF16), b.astype(BF16), preferred_element_type=F32)


def _bdot_nt(a, b):
    return lax.dot_general(a.astype(BF16), b.astype(BF16), (((1,), (1,)), ((), ())),
                           preferred_element_type=F32)


def _bdot_tn(a, b):
    return lax.dot_general(a.astype(BF16), b.astype(BF16), (((0,), (0,)), ((), ())),
                           preferred_element_type=F32)


def _big(x, lo_half):
    zero = jnp.zeros_like(x)
    return jnp.concatenate([jnp.where(lo_half, x, zero), jnp.where(lo_half, zero, x)], axis=0)


def _fold(x):
    c = x.shape[0] // 2
    return x[:c] + x[c:]


def _scan_prep_kernel(r_ref, lw_ref, k_ref, v_ref, a_ref, kkw_ref, kaw_ref, rkw_ref,
                      p_ref, m_ref, y1_ref, n_ref, dec_ref, bonus_ref, *, npairs):
    c = CHUNK
    c2 = 2 * c
    row = lax.broadcasted_iota(jnp.int32, (c2, c2), 0)
    col = lax.broadcasted_iota(jnp.int32, (c2, c2), 1)
    strict = (row % c) > (col % c)
    incl = (row % c) >= (col % c)
    same_blk = (row // SUB) == (col // SUB)
    eye = jnp.where(row == col, 1.0, 0.0).astype(F32)
    lo_half = lax.broadcasted_iota(jnp.int32, (c, c2), 1) < HEAD
    tri = lax.broadcasted_iota(jnp.int32, (c, c), 0) >= lax.broadcasted_iota(jnp.int32, (c, c), 1)
    ones_tril = jnp.where(tri, 1.0, 0.0).astype(F32)

    lw = lw_ref[0]
    g = jnp.dot(ones_tril, lw, preferred_element_type=F32, precision=lax.Precision.HIGHEST)
    gc = g[c - 1:c, :]
    eg = jnp.exp(g)
    eng = jnp.exp(-g)
    egc = jnp.exp(gc - g)
    r_all = r_ref[0]
    a_all = a_ref[0]
    v_all = v_ref[0]
    kkr = k_ref[0] * kkw_ref[...]
    k_all = k_ref[0] * (1.0 + (a_all - 1.0) * kaw_ref[...])
    kka = kkr * a_all
    kk_t = kkr * jnp.exp(g - lw)
    r_t = r_all * eg
    k_h = k_all * eng
    b_h = kka * eng
    k_g = k_all * egc
    b_g = kka * egc
    rk = r_all * k_all * rkw_ref[...]
    dec_ref[0, 0] = jnp.exp(gc)

    units = range(npairs)

    def pair(x, u):
        return _big(x[:, u * c2:(u + 1) * c2], lo_half)

    kk_rs = [lax.rsqrt(jnp.maximum(jnp.sum(jnp.square(pair(kkr, u)), axis=1, keepdims=True), KK_EPS))
             for u in units]
    kkt = [pair(kk_t, u) * kk_rs[u] for u in units]
    rt = [pair(r_t, u) for u in units]
    vb = [pair(v_all, u) for u in units]
    bg = [pair(b_g, u) * kk_rs[u] for u in units]
    for u in units:
        bonus_ref[0, :, u * c2:(u + 1) * c2] = _fold(jnp.sum(pair(rk, u), axis=1, keepdims=True) * vb[u])
    gm = [_bdot_nt(jnp.concatenate([kkt[u], rt[u]], axis=0),
                   jnp.concatenate([pair(b_h, u) * kk_rs[u], pair(k_h, u)], axis=0)) for u in units]
    a_b = [jnp.where(strict, gm[u][:c2, :c2], 0.0) for u in units]
    a_k = [jnp.where(strict, gm[u][:c2, c2:], 0.0) for u in units]
    a_rb = [jnp.where(incl, gm[u][c2:, :c2], 0.0) for u in units]
    a_rk = [jnp.where(incl, gm[u][c2:, c2:], 0.0) for u in units]
    x2 = [_bdot(jnp.concatenate([a_k[u], a_rk[u]], axis=0), vb[u]) for u in units]

    d = [jnp.where(same_blk, a_b[u], 0.0) for u in units]
    low = [a_b[u] - d[u] for u in units]
    p = [eye - d[u] for u in units]
    dk = [_bdot(d[u], d[u]) for u in units]
    levels = int(math.log2(SUB)) - 1
    for lvl in range(levels):
        if lvl < levels - 1:
            both = [_bdot(jnp.concatenate([p[u], dk[u]], axis=0), dk[u]) for u in units]
            p = [p[u] + both[u][:c2] for u in units]
            dk = [both[u][c2:] for u in units]
        else:
            p = [p[u] + _bdot(p[u], dk[u]) for u in units]
    e = [_bdot(p[u], low[u]) for u in units]
    q = [eye - e[u] for u in units]
    ek = e
    for _ in range(int(math.log2(CHUNK // SUB)) - 1):
        ek = [_bdot(ek[u], ek[u]) for u in units]
        q = [q[u] + _bdot(q[u], ek[u]) for u in units]
    t_inv = [_bdot(q[u], p[u]) for u in units]

    wu = [_bdot(t_inv[u], jnp.concatenate([kkt[u], x2[u][:c2]], axis=1)) for u in units]
    aw = [_bdot(a_rb[u], wu[u]) for u in units]
    m0 = [_bdot_tn(-bg[u], wu[u][:, :c2]) for u in units]
    nt = [_bdot_tn(jnp.concatenate([vb[u], wu[u][:, c2:]], axis=0),
                   jnp.concatenate([pair(k_g, u), -bg[u]], axis=0)) for u in units]
    for u in units:
        sl = slice(u * c2, (u + 1) * c2)
        p_ref[0, :, sl] = _fold(rt[u] - aw[u][:, :c2]).astype(p_ref.dtype)
        y1_ref[0, :, sl] = _fold(x2[u][c2:] - aw[u][:, c2:])
        m_ref[0, :, sl] = _fold(m0[u]).astype(m_ref.dtype)
        n_ref[0, :, sl] = _fold(nt[u])


def _scan_seq_kernel(p_ref, m_ref, y1_ref, n_ref, dec_ref, bonus_ref, g_ref, lnw_ref, lnb_ref,
                     o_ref, s_ref, *, npairs):
    c2 = 2 * CHUNK

    @pl.when(pl.program_id(2) == 0)
    def _():
        s_ref[...] = jnp.zeros_like(s_ref)

    lo_half = lax.broadcasted_iota(jnp.int32, (CHUNK, c2), 1) < HEAD
    own = ((lax.broadcasted_iota(jnp.int32, (c2, c2), 0) < CHUNK)
           == (lax.broadcasted_iota(jnp.int32, (c2, c2), 1) < HEAD))
    for u in range(npairs):
        sl = slice(u * c2, (u + 1) * c2)
        s = s_ref[u]
        y = _bdot_nt(_big(p_ref[0, :, sl], lo_half), s) + _big(y1_ref[0, :, sl], lo_half)
        s_ref[u] = (s * dec_ref[0, 0, :, sl] + _bdot_nt(s, _big(m_ref[0, :, sl], lo_half))
                    + _big(n_ref[0, :, sl], lo_half))
        mu = jnp.sum(y, axis=1, keepdims=True) * (1.0 / HEAD)
        yc = jnp.where(own, y - mu, 0.0)
        var = jnp.sum(yc * yc, axis=1, keepdims=True) * (1.0 / HEAD)
        yn = _fold(yc * lax.rsqrt(var + GN_EPS)) * lnw_ref[:, sl] + lnb_ref[:, sl]
        o_ref[0, :, sl] = ((yn + bonus_ref[0, :, sl]) * g_ref[0, :, sl]).astype(o_ref.dtype)


def _rwkv_mix(r, lw, k, v, a, g, k_k, k_a, r_k, lnx_w, lnx_b):
    bsz, tp, d = r.shape
    nc = tp // CHUNK
    lanes = _divisor(d, 2048, 128)
    spec = pl.BlockSpec((1, CHUNK, lanes), lambda i, j, c: (i, c, j))
    dspec = pl.BlockSpec((1, 1, 1, lanes), lambda i, j, c: (i, c, 0, j))
    rowspec = pl.BlockSpec((1, lanes), lambda i, j, c: (0, j))
    tile = jax.ShapeDtypeStruct((bsz, tp, d), F32)
    tile16 = jax.ShapeDtypeStruct((bsz, tp, d), BF16)
    p, m0, y1, nt, dec, bonus = pl.pallas_call(
        functools.partial(_scan_prep_kernel, npairs=lanes // 128),
        grid=(bsz, d // lanes, nc),
        in_specs=[spec] * 5 + [rowspec] * 3,
        out_specs=[spec, spec, spec, spec, dspec, spec],
        out_shape=[tile16, tile16, tile, tile, jax.ShapeDtypeStruct((bsz, nc, 1, d), F32), tile],
        compiler_params=_params("parallel", "parallel", "parallel"),
        name="rwkv_scan_prep",
    )(r, lw, k, v, a, k_k.reshape(1, d), k_a.reshape(1, d), r_k.reshape(1, d))
    lanes = _divisor(d, 2048, 128)
    spec = pl.BlockSpec((1, CHUNK, lanes), lambda i, j, c: (i, c, j))
    dspec = pl.BlockSpec((1, 1, 1, lanes), lambda i, j, c: (i, c, 0, j))
    rowspec = pl.BlockSpec((1, lanes), lambda i, j, c: (0, j))
    return pl.pallas_call(
        functools.partial(_scan_seq_kernel, npairs=lanes // 128),
        grid=(bsz, d // lanes, nc),
        in_specs=[spec, spec, spec, spec, dspec, spec, spec, rowspec, rowspec],
        out_specs=spec,
        out_shape=tile16,
        scratch_shapes=[pltpu.VMEM((lanes // 128, 2 * CHUNK, 2 * CHUNK), F32)],
        compiler_params=_params("parallel", "parallel", "arbitrary"),
        name="rwkv_scan_seq",
    )(p, m0, y1, nt, dec, bonus, g, lnx_w.reshape(1, d), lnx_b.reshape(1, d))


HALO = 32


CONV_STRIP = 512


def _conv_kernel(prev_ref, cur_ref, w_ref, b_ref, lnw_ref, lnb_ref, o_ref, buf_ref, acc_ref, sh_ref, *, width, tt, strip):
    first = pl.program_id(1) == 0
    buf_ref[0:HALO, :] = jnp.where(first, 0.0, prev_ref[0])
    buf_ref[HALO:HALO + tt, :] = cur_ref[0]
    off = HALO - (width - 1)
    for s in range(buf_ref.shape[1] // strip):
        lanes = slice(s * strip, (s + 1) * strip)
        acc = jnp.zeros((tt, strip), F32) + b_ref[:, lanes]
        for res in range(SUBLANES):
            span = tt + HALO - (SUBLANES if res else 0)
            if res:
                sh_ref[0:span, :] = buf_ref[res:res + span, lanes]
            for base in range(0, span - tt + 1, SUBLANES):
                j = base + res - off
                if 0 <= j < width:
                    src = sh_ref[base:base + tt, :] if res else buf_ref[base:base + tt, lanes]
                    acc = acc + src * w_ref[j:j + 1, lanes]
        acc_ref[:, lanes] = acc
    z = acc_ref[...]
    mu = jnp.mean(z, axis=-1, keepdims=True)
    zc = z - mu
    var = jnp.mean(zc * zc, axis=-1, keepdims=True)
    zn = zc * lax.rsqrt(var + LN_EPS) * lnw_ref[...] + lnb_ref[...]
    o_ref[0] = (zn * jax.nn.sigmoid(zn)).astype(o_ref.dtype)


def _dwconv_ln_swish(x, w, bias, ln_w, ln_b):
    bsz, tp, d = x.shape
    width = w.shape[0]
    assert width - 1 <= HALO
    tt = _divisor(tp, 320, HALO)
    ratio = tt // HALO
    wpad = jnp.pad(w, ((0, HALO - width), (0, 0)))
    row = pl.BlockSpec((1, d), lambda i, t: (0, 0))
    strip = _divisor(d, CONV_STRIP, 128)
    return pl.pallas_call(
        functools.partial(_conv_kernel, width=width, tt=tt, strip=strip),
        grid=(bsz, tp // tt),
        in_specs=[
            pl.BlockSpec((1, HALO, d), lambda i, t: (i, jnp.maximum(t * ratio - 1, 0), 0)),
            pl.BlockSpec((1, tt, d), lambda i, t: (i, t, 0)),
            pl.BlockSpec((HALO, d), lambda i, t: (0, 0)),
            row, row, row,
        ],
        out_specs=pl.BlockSpec((1, tt, d), lambda i, t: (i, t, 0)),
        out_shape=jax.ShapeDtypeStruct((bsz, tp, d), BF16),
        scratch_shapes=[pltpu.VMEM((HALO + tt, d), F32), pltpu.VMEM((tt, d), F32),
                        pltpu.VMEM((HALO + tt, strip), F32)],
        compiler_params=_params("parallel", "parallel"),
        name="dwconv_ln_swish",
    )(x, x, wpad, bias.reshape(1, d), ln_w.reshape(1, d), ln_b.reshape(1, d))


NEG = -jnp.inf
N_CAND = 16 + 7 * 8 + 8


def _cand_static():
    row = lax.broadcasted_iota(jnp.int32, (N_CAND, 1), 0)
    grp = jnp.where(row < 16, 0, (row - 16) // 8 + 1)
    bmid = (row - 16) % 8
    last = row >= 72
    a = jnp.where(row < 16, 0, jnp.where(last, row - 64, grp))
    b = jnp.where(row < 16, row, jnp.where(last, 0, bmid))
    valid = (a + 1) * (b + 1) <= PEER_TOPK
    return a * PEER_TOPK + b, valid


def _retrieve_kernel(q_ref, keys_ref, g_ref, i_ref, j_ref,
                     sv_ref, si_ref, cs_ref, cc_ref, ts_ref, tc_ref, kiota_ref, *, nheads, nkeys, tt):
    kiota_ref[...] = lax.broadcasted_iota(jnp.int32, (nkeys, tt), 0).astype(F32)
    flat, valid = _cand_static()
    code0 = (flat * (nkeys * nkeys)).astype(F32)
    big = float(1 << 24)

    def sub_keys(h):
        slot = h % 2
        s = []
        for half in range(2):
            hc = h * 2 + half
            q = q_ref[hc * nkeys:(hc + 1) * nkeys, :]
            s.append(jnp.dot(keys_ref[hc], q.astype(BF16), preferred_element_type=F32))
        for r in range(PEER_TOPK):
            for half in range(2):
                m = jnp.max(s[half], axis=0, keepdims=True)
                idx = jnp.min(jnp.where(s[half] == m, kiota_ref[...], big), axis=0, keepdims=True)
                sv_ref[slot, half, r:r + 1, :] = m
                si_ref[slot, half, r:r + 1, :] = idx
                s[half] = jnp.where(kiota_ref[...] == idx, NEG, s[half])

    def experts(h):
        sv = sv_ref.at[h % 2]
        si = si_ref.at[h % 2]
        cs_ref[0:16, :] = sv[0, 0:1, :] + sv[1, :, :]
        cc_ref[0:16, :] = si[0, 0:1, :] * nkeys + si[1, :, :]
        for a in range(1, 8):
            lo = 16 + (a - 1) * 8
            cs_ref[lo:lo + 8, :] = sv[0, a:a + 1, :] + sv[1, 0:8, :]
            cc_ref[lo:lo + 8, :] = si[0, a:a + 1, :] * nkeys + si[1, 0:8, :]
        cs_ref[72:80, :] = sv[0, 8:16, :] + sv[1, 0:1, :]
        cc_ref[72:80, :] = si[0, 8:16, :] * nkeys + si[1, 0:1, :]
        cand = jnp.where(valid, cs_ref[...], NEG)
        cc_ref[...] = cc_ref[...] + code0
        for r in range(PEER_TOPK):
            m = jnp.max(cand, axis=0, keepdims=True)
            c = jnp.min(jnp.where(cand == m, cc_ref[...], big), axis=0, keepdims=True)
            ts_ref[r:r + 1, :] = m
            tc_ref[r:r + 1, :] = c
            cand = jnp.where(cc_ref[...] == c, NEG, cand)
        rows = slice(h * PEER_TOPK, (h + 1) * PEER_TOPK)
        top = ts_ref[...]
        ex = jnp.exp(top - jnp.max(top, axis=0, keepdims=True))
        g_ref[rows, :] = ex / jnp.sum(ex, axis=0, keepdims=True)
        expert = tc_ref[...].astype(jnp.int32) % (nkeys * nkeys)
        i_ref[rows, :] = expert // nkeys
        j_ref[rows, :] = expert % nkeys

    sub_keys(0)
    for h in range(nheads):
        if h + 1 < nheads:
            sub_keys(h + 1)
        experts(h)


def _peer_retrieve(qt, keys):
    ph, _, nkeys, dk = keys.shape
    m = qt.shape[1]
    assert nkeys == dk == 128
    tt = 128
    keys2 = keys.reshape(ph * 2, nkeys, dk).astype(BF16)
    np_ = ph * PEER_TOPK
    out_spec = pl.BlockSpec((np_, tt), lambda i: (0, i))
    return pl.pallas_call(
        functools.partial(_retrieve_kernel, nheads=ph, nkeys=nkeys, tt=tt),
        grid=(m // tt,),
        in_specs=[
            pl.BlockSpec((ph * 2 * dk, tt), lambda i: (0, i)),
            pl.BlockSpec((ph * 2, nkeys, dk), lambda i: (0, 0, 0)),
        ],
        out_specs=[out_spec, out_spec, out_spec],
        out_shape=[
            jax.ShapeDtypeStruct((np_, m), F32),
            jax.ShapeDtypeStruct((np_, m), jnp.int32),
            jax.ShapeDtypeStruct((np_, m), jnp.int32),
        ],
        scratch_shapes=[
            pltpu.VMEM((2, 2, PEER_TOPK, tt), F32),
            pltpu.VMEM((2, 2, PEER_TOPK, tt), F32),
            pltpu.VMEM((N_CAND, tt), F32),
            pltpu.VMEM((N_CAND, tt), F32),
            pltpu.VMEM((PEER_TOPK, tt), F32),
            pltpu.VMEM((PEER_TOPK, tt), F32),
            pltpu.VMEM((nkeys, tt), F32),
        ],
        compiler_params=_params("parallel"),
        name="peer_retrieve",
    )(qt, keys2)


DENSE_SUB = 512


def _gate_table_kernel(g_ref, i_ref, j_ref, o_ref, *, tb, nkeys):
    sub = lax.broadcasted_iota(jnp.int32, (nkeys, g_ref.shape[1]), 0)

    def body(t, carry):
        g = g_ref[pl.ds(t, 1), :]
        i = i_ref[pl.ds(t, 1), :]
        j = j_ref[pl.ds(t, 1), :]
        a_t = jnp.where(i == sub, g, 0.0).astype(BF16)
        b_t = jnp.where(j == sub, 1.0, 0.0).astype(BF16)
        w = lax.dot_general(a_t, b_t, (((1,), (1,)), ((), ())), preferred_element_type=F32)
        o_ref[:, pl.ds(t, 1)] = w.reshape(nkeys // SUBLANES, 1, SUBLANES, nkeys)
        return carry

    lax.fori_loop(0, tb, body, 0, unroll=64)


def _gate_table(g, i, j, nkeys):
    m, p = g.shape
    tb = _divisor(m, 64, 8)
    spec = pl.BlockSpec((tb, p), lambda t: (t, 0))
    ng = nkeys // SUBLANES
    w = pl.pallas_call(
        functools.partial(_gate_table_kernel, tb=tb, nkeys=nkeys),
        grid=(m // tb,),
        in_specs=[spec, spec, spec],
        out_specs=pl.BlockSpec((ng, tb, SUBLANES, nkeys), lambda t: (0, t, 0, 0)),
        out_shape=jax.ShapeDtypeStruct((ng, m, SUBLANES, nkeys), F32),
        compiler_params=_params("parallel"),
        name="peer_gate_table",
    )(g, i, j)
    return w.reshape(ng, m * SUBLANES, nkeys)


def _peer_dense_kernel(*refs, tm, nkeys, final_norm):
    if final_norm:
        h_ref, g_ref, u_ref, v_ref, w_ref, gf_ref, o_ref, x_ref = refs
    else:
        h_ref, g_ref, u_ref, v_ref, w_ref, o_ref, x_ref = refs
    k = pl.program_id(1)

    @pl.when(k == 0)
    def _():
        h = h_ref[...]
        o_ref[...] = h
        x_ref[...] = _rms(h, g_ref[...]).astype(BF16)

    per_sub = DENSE_SUB // nkeys
    for s in range(SUBLANES // per_sub):
        cols = slice(s * DENSE_SUB, (s + 1) * DENSE_SUB)
        act = lax.dot_general(x_ref[...], u_ref[cols, :], (((1,), (1,)), ((), ())),
                              preferred_element_type=F32)
        gelu = 0.5 * act * (1.0 + lax.erf(act * (1.0 / math.sqrt(2.0))))
        w = jnp.concatenate(
            [w_ref[pl.ds(s * per_sub + i, tm, stride=SUBLANES), :] for i in range(per_sub)], axis=1)
        o_ref[...] += jnp.dot((w * gelu).astype(BF16), v_ref[cols, :], preferred_element_type=F32)

    if final_norm:
        @pl.when(k == pl.num_programs(1) - 1)
        def _():
            o_ref[...] = _rms(o_ref[...], gf_ref[...])


def _peer_dense(h, norm_g, u, v, w, final_g=None):
    m, d = h.shape
    ng, _, nkeys = w.shape
    assert DENSE_SUB % nkeys == 0 and SUBLANES * nkeys % DENSE_SUB == 0
    te = SUBLANES * nkeys
    tm = _divisor(m, 832, 64)
    row = pl.BlockSpec((1, d), lambda i, k: (0, 0))
    args = [h, norm_g.reshape(1, d), u, v, w]
    specs = [
        pl.BlockSpec((tm, d), lambda i, k: (i, 0), pipeline_mode=pl.Buffered(1)),
        row,
        pl.BlockSpec((te, d), lambda i, k: (k, 0)),
        pl.BlockSpec((te, d), lambda i, k: (k, 0)),
        pl.BlockSpec((None, tm * SUBLANES, nkeys), lambda i, k: (k, i, 0)),
    ]
    if final_g is not None:
        args.append(final_g.reshape(1, d))
        specs.append(row)
    return pl.pallas_call(
        functools.partial(_peer_dense_kernel, tm=tm, nkeys=nkeys, final_norm=final_g is not None),
        grid=(m // tm, ng),
        in_specs=specs,
        out_specs=pl.BlockSpec((tm, d), lambda i, k: (i, 0)),
        out_shape=jax.ShapeDtypeStruct((m, d), F32),
        scratch_shapes=[pltpu.VMEM((tm, d), BF16)],
        compiler_params=_params("parallel", "arbitrary"),
        name="peer_dense",
    )(*args)


def _peer_layer(h, g_norm, wq, keys, u_tab, v_tab, final_g=None):
    nkeys = keys.shape[2]
    qt = _proj_t(wq.T.astype(BF16), h, g_norm)
    gate, ii, jj = _peer_retrieve(qt, keys)
    w = _gate_table(gate.T, ii.T, jj.T, nkeys)
    return _peer_dense(h, g_norm, u_tab.astype(BF16), v_tab.astype(BF16), w, final_g)


def _rwkv_layer(h, tp, g_norm, mix, w0, w1, w2, a0, a1, a2, g1, g2, k_k, k_a, r_k, w_rkv, w_o, lnx_w, lnx_b):
    m, d = h.shape
    bsz = m // tp
    nh = d // HEAD

    def shifted(n, weights, **kw):
        return _proj(h, weights, norm_g=g_norm, mix=mix[n], seq_len=tp, **kw)

    r = shifted(0, (w_rkv[0], None))
    k = shifted(2, (w_rkv[1], None))
    v = shifted(3, (w_rkv[2], None))
    lw, a, g = _lora3(h, g_norm, jnp.stack([mix[1], mix[4], mix[5]]), tp,
                      [(w1, w2, w0), (a1, a2, a0), (g1, g2, None)],
                      mids=("tanh", None, "sigmoid"), posts=("log_decay", "sigmoid", None))

    def seq(z):
        return z.reshape(bsz, tp, d)

    out = _rwkv_mix(seq(r), seq(lw), seq(k), seq(v), seq(a), seq(g), k_k, k_a, r_k, lnx_w, lnx_b)
    return _proj(out.reshape(m, d), (w_o, None), residual=h)


def _conv_layer(h, tp, g_norm, pw1_w, pw1_b, dw_w, dw_b, ln_w, ln_b, pw2_w, pw2_b):
    m, d = h.shape
    z = _proj(h, (pw1_w, pw1_b), body="glu", norm_g=g_norm)
    z = _dwconv_ln_swish(z.reshape(m // tp, tp, d), dw_w, dw_b, ln_w, ln_b).reshape(m, d)
    return _proj(z, (pw2_w, pw2_b), residual=h)


def kernel(x, meta_tokens, norm_mix_a, rwkv_mix, rwkv_w0, rwkv_w1, rwkv_w2, rwkv_a0, rwkv_a1, rwkv_a2, rwkv_g1, rwkv_g2, rwkv_k_k, rwkv_k_a, rwkv_r_k, rwkv_w_rkv, rwkv_w_o, rwkv_lnx_w, rwkv_lnx_b, norm_mix_b, conv_pw1_w, conv_pw1_b, conv_dw_w, conv_dw_b, conv_ln_w, conv_ln_b, conv_pw2_w, conv_pw2_b, norm_ffn, peer_wq, peer_keys, peer_u, peer_v, norm_final):
    bsz, seq, d = x.shape
    n_meta = meta_tokens.shape[0]
    t = n_meta + seq
    tp = -(-t // 128) * 128 if t <= 128 else -(-t // CHUNK) * CHUNK
    depth = norm_ffn.shape[0]
    meta = jnp.broadcast_to(meta_tokens[None].astype(x.dtype), (bsz, n_meta, d))
    h = jnp.concatenate([meta, x, jnp.zeros((bsz, tp - t, d), x.dtype)], axis=1).reshape(bsz * tp, d)
    for i in range(depth):
        j = i // 2
        if i % 2 == 0:
            h = _rwkv_layer(
                h, tp, norm_mix_a[j], rwkv_mix[j], rwkv_w0[j], rwkv_w1[j], rwkv_w2[j], rwkv_a0[j], rwkv_a1[j],
                rwkv_a2[j], rwkv_g1[j], rwkv_g2[j], rwkv_k_k[j], rwkv_k_a[j], rwkv_r_k[j], rwkv_w_rkv[j],
                rwkv_w_o[j], rwkv_lnx_w[j], rwkv_lnx_b[j])
        else:
            h = _conv_layer(
                h, tp, norm_mix_b[j], conv_pw1_w[j], conv_pw1_b[j], conv_dw_w[j], conv_dw_b[j], conv_ln_w[j],
                conv_ln_b[j], conv_pw2_w[j], conv_pw2_b[j])
        h = _peer_layer(h, norm_ffn[i], peer_wq[i], peer_keys[i], peer_u[i], peer_v[i],
                        norm_final if i == depth - 1 else None)
    if depth == 0:
        h = _rms(h, norm_final)
    return h.reshape(bsz, tp, d)[:, n_meta:t]
```

```python
import functools
import math

import jax
import jax.numpy as jnp
from jax import lax
from jax.experimental import pallas as pl
from jax.experimental.pallas import tpu as pltpu

F32 = jnp.float32
BF16 = jnp.bfloat16

NORM_EPS = 1e-6
GN_EPS = 64e-5
KK_EPS = 1e-24
LN_EPS = 1e-5
HEAD = 64
CHUNK = 64
SUB = 16
PEER_TOPK = 16
SUBLANES = 8
VMEM_LIMIT = 56 * 1024 * 1024


def _divisor(n, target, mult):
    best = None
    for d in range(mult, min(n, target) + 1, mult):
        if n % d == 0:
            best = d
    assert best is not None, (n, target, mult)
    return best


def _params(*sem):
    return pltpu.CompilerParams(dimension_semantics=sem, vmem_limit_bytes=VMEM_LIMIT)


def _rms(x, g):
    return x * lax.rsqrt(jnp.mean(x * x, axis=-1, keepdims=True) + NORM_EPS) * g


def _proj_kernel(*refs, prologue, body, mid, post, residual, tm, seq_tiles):
    it = iter(refs)
    a_ref = next(it)
    prev_ref = next(it) if prologue == "shift" else None
    g_ref = next(it) if prologue != "none" else None
    mix_ref = next(it) if prologue == "shift" else None
    w_refs = [next(it) for _ in range({"mm": 2, "glu": 4, "lora": 3}[body])]
    res_ref = next(it) if residual else None
    o_ref = next(it)
    buf_ref = next(it) if prologue == "shift" else None

    if prologue == "none":
        a = a_ref[...]
    else:
        x = _rms(a_ref[...], g_ref[...])
        if prologue == "shift":
            first = (pl.program_id(1) % seq_tiles) == 0
            buf_ref[0:SUBLANES, :] = jnp.where(first, 0.0, _rms(prev_ref[...], g_ref[...]))
            buf_ref[SUBLANES:SUBLANES + tm, :] = x
            x = x + (buf_ref[SUBLANES - 1:SUBLANES - 1 + tm, :] - x) * mix_ref[...]
        a = x.astype(BF16)

    if body == "mm":
        w_ref, b_ref = w_refs
        z = jnp.dot(a, w_ref[...], preferred_element_type=F32) + b_ref[...]
    elif body == "glu":
        wl_ref, wr_ref, bl_ref, br_ref = w_refs
        z = ((jnp.dot(a, wl_ref[...], preferred_element_type=F32) + bl_ref[...])
             * jax.nn.sigmoid(jnp.dot(a, wr_ref[...], preferred_element_type=F32) + br_ref[...]))
    else:
        w1_ref, w2_ref, b_ref = w_refs
        hid = jnp.dot(a, w1_ref[...], preferred_element_type=F32)
        if mid == "tanh":
            hid = jnp.tanh(hid)
        elif mid == "sigmoid":
            hid = jax.nn.sigmoid(hid)
        z = jnp.dot(hid.astype(BF16), w2_ref[...], preferred_element_type=F32) + b_ref[...]
    if post == "log_decay":
        z = -jnp.exp(-jax.nn.softplus(-z) - 0.5)
    elif post == "sigmoid":
        z = jax.nn.sigmoid(z)
    if residual:
        z = z + res_ref[...]
    o_ref[...] = z.astype(o_ref.dtype)


def _proj(a, weights, *, body="mm", norm_g=None, mix=None, seq_len=None, mid=None, post=None,
          residual=None, out_dtype=F32):
    m, k = a.shape
    prologue = "none" if norm_g is None else ("norm" if mix is None else "shift")
    if body == "lora":
        w1, w2, bias = weights
        r = w1.shape[1]
        rp = -(-r // 128) * 128
        n = w2.shape[1]
        tn = n
        ws = [jnp.pad(w1, ((0, 0), (0, rp - r))).astype(BF16), jnp.pad(w2, ((0, rp - r), (0, 0))).astype(BF16)]
        w_specs = [pl.BlockSpec((k, rp), lambda j, i: (0, 0)), pl.BlockSpec((rp, n), lambda j, i: (0, 0))]
        biases = [bias]
    elif body == "glu":
        w, bias = weights
        n = w.shape[1] // 2
        tn = _divisor(n, 1024, 128)
        ws = [w[:, :n].astype(BF16), w[:, n:].astype(BF16)]
        w_specs = [pl.BlockSpec((k, tn), lambda j, i: (0, j))] * 2
        biases = [bias[:n], bias[n:]]
    else:
        w, bias = weights
        n = w.shape[1]
        tn = _divisor(n, 2048, 128)
        ws = [w.astype(BF16)]
        w_specs = [pl.BlockSpec((k, tn), lambda j, i: (0, j), pipeline_mode=pl.Buffered(1))]
        biases = [bias]
    biases = [(jnp.zeros((n,), F32) if b is None else b).reshape(1, n).astype(F32) for b in biases]
    b_specs = [pl.BlockSpec((1, tn), lambda j, i: (0, j))] * len(biases)

    rows = m if seq_len is None else seq_len
    tm = _divisor(rows, 448, 16)
    args = [a]
    specs = [pl.BlockSpec((tm, k), lambda j, i: (i, 0))]
    scratch = []
    if prologue == "shift":
        ratio = tm // SUBLANES
        args.append(a)
        specs.append(pl.BlockSpec((SUBLANES, k), lambda j, i: (jnp.maximum(i * ratio - 1, 0), 0)))
        scratch = [pltpu.VMEM((SUBLANES + tm, k), F32)]
    if prologue != "none":
        args.append(norm_g.reshape(1, k))
        specs.append(pl.BlockSpec((1, k), lambda j, i: (0, 0)))
    if prologue == "shift":
        args.append(mix.reshape(1, k))
        specs.append(pl.BlockSpec((1, k), lambda j, i: (0, 0)))
    args += ws + biases
    specs += w_specs + b_specs
    if residual is not None:
        args.append(residual)
        specs.append(pl.BlockSpec((tm, tn), lambda j, i: (i, j)))
    return pl.pallas_call(
        functools.partial(_proj_kernel, prologue=prologue, body=body, mid=mid, post=post,
                          residual=residual is not None, tm=tm, seq_tiles=rows // tm),
        grid=(n // tn, m // tm),
        in_specs=specs,
        out_specs=pl.BlockSpec((tm, tn), lambda j, i: (i, j)),
        out_shape=jax.ShapeDtypeStruct((m, n), out_dtype),
        scratch_shapes=scratch,
        compiler_params=_params("parallel", "parallel"),
        name="proj_" + prologue + "_" + body,
    )(*args)


def _lora3_kernel(a_ref, prev_ref, g_ref, mix_ref, *refs, tm, seq_tiles, mids, posts):
    n = len(mids)
    w_refs, o_refs, buf_ref = refs[:3 * n], refs[3 * n:4 * n], refs[4 * n]
    x = _rms(a_ref[...], g_ref[...])
    first = (pl.program_id(0) % seq_tiles) == 0
    buf_ref[0:SUBLANES, :] = jnp.where(first, 0.0, _rms(prev_ref[...], g_ref[...]))
    buf_ref[SUBLANES:SUBLANES + tm, :] = x
    dx = buf_ref[SUBLANES - 1:SUBLANES - 1 + tm, :] - x
    for i in range(n):
        w1_ref, w2_ref, b_ref = w_refs[3 * i:3 * i + 3]
        a = (x + dx * mix_ref[i:i + 1, :]).astype(BF16)
        hid = jnp.dot(a, w1_ref[...], preferred_element_type=F32)
        if mids[i] == "tanh":
            hid = jnp.tanh(hid)
        elif mids[i] == "sigmoid":
            hid = jax.nn.sigmoid(hid)
        z = jnp.dot(hid.astype(BF16), w2_ref[...], preferred_element_type=F32) + b_ref[...]
        if posts[i] == "log_decay":
            z = -jnp.exp(-jax.nn.softplus(-z) - 0.5)
        elif posts[i] == "sigmoid":
            z = jax.nn.sigmoid(z)
        o_refs[i][...] = z


def _lora3(h, norm_g, mixes, seq_len, loras, mids, posts):
    m, k = h.shape
    tm = _divisor(seq_len, 448, 16)
    ratio = tm // SUBLANES
    const = lambda i: (0, 0)
    args = [h, h, norm_g.reshape(1, k), mixes]
    specs = [pl.BlockSpec((tm, k), lambda i: (i, 0)),
             pl.BlockSpec((SUBLANES, k), lambda i: (jnp.maximum(i * ratio - 1, 0), 0)),
             pl.BlockSpec((1, k), const), pl.BlockSpec(mixes.shape, const)]
    n = None
    for w1, w2, bias in loras:
        r, n = w1.shape[1], w2.shape[1]
        rp = -(-r // 128) * 128
        args += [jnp.pad(w1, ((0, 0), (0, rp - r))).astype(BF16), jnp.pad(w2, ((0, rp - r), (0, 0))).astype(BF16),
                 (jnp.zeros((n,), F32) if bias is None else bias).reshape(1, n).astype(F32)]
        specs += [pl.BlockSpec((k, rp), const), pl.BlockSpec((rp, n), const), pl.BlockSpec((1, n), const)]
    return pl.pallas_call(
        functools.partial(_lora3_kernel, tm=tm, seq_tiles=seq_len // tm, mids=mids, posts=posts),
        grid=(m // tm,),
        in_specs=specs,
        out_specs=[pl.BlockSpec((tm, n), lambda i: (i, 0))] * len(loras),
        out_shape=[jax.ShapeDtypeStruct((m, n), F32)] * len(loras),
        scratch_shapes=[pltpu.VMEM((SUBLANES + tm, k), F32)],
        compiler_params=_params("parallel"),
        name="lora3",
    )(*args)


def _proj_t_kernel(w_ref, a_ref, g_ref, o_ref):
    a = _rms(a_ref[...], g_ref[...]).astype(BF16)
    o_ref[...] = lax.dot_general(w_ref[...], a, (((1,), (1,)), ((), ())), preferred_element_type=F32)


def _proj_t(wt, a, norm_g):
    n, k = wt.shape
    m = a.shape[0]
    tm = _divisor(m, 1024, 128)
    tn = _divisor(n, 2048, 128)
    return pl.pallas_call(
        _proj_t_kernel,
        grid=(n // tn, m // tm),
        in_specs=[
            pl.BlockSpec((tn, k), lambda j, i: (j, 0), pipeline_mode=pl.Buffered(1)),
            pl.BlockSpec((tm, k), lambda j, i: (i, 0)),
            pl.BlockSpec((1, k), lambda j, i: (0, 0)),
        ],
        out_specs=pl.BlockSpec((tn, tm), lambda j, i: (j, i)),
        out_shape=jax.ShapeDtypeStruct((n, m), F32),
        compiler_params=_params("parallel", "parallel"),
        name="proj_t",
    )(wt, a, norm_g.reshape(1, k))


def _bdot(a, b):
    return jnp.dot(a.astype(BF16), b.astype(BF16), preferred_element_type=F32)


def _bdot_nt(a, b):
    return lax.dot_general(a.astype(BF16), b.astype(BF16), (((1,), (1,)), ((), ())),
                           preferred_element_type=F32)


def _bdot_tn(a, b):
    return lax.dot_general(a.astype(BF16), b.astype(BF16), (((0,), (0,)), ((), ())),
                           preferred_element_type=F32)


def _big(x, lo_half):
    zero = jnp.zeros_like(x)
    return jnp.concatenate([jnp.where(lo_half, x, zero), jnp.where(lo_half, zero, x)], axis=0)


def _fold(x):
    c = x.shape[0] // 2
    return x[:c] + x[c:]


def _scan_prep_kernel(r_ref, lw_ref, k_ref, v_ref, a_ref, kkw_ref, kaw_ref, rkw_ref,
                      p_ref, m_ref, y1_ref, n_ref, dec_ref, bonus_ref, *, npairs):
    c = CHUNK
    c2 = 2 * c
    row = lax.broadcasted_iota(jnp.int32, (c2, c2), 0)
    col = lax.broadcasted_iota(jnp.int32, (c2, c2), 1)
    strict = (row % c) > (col % c)
    incl = (row % c) >= (col % c)
    same_blk = (row // SUB) == (col // SUB)
    eye = jnp.where(row == col, 1.0, 0.0).astype(F32)
    lo_half = lax.broadcasted_iota(jnp.int32, (c, c2), 1) < HEAD
    tri = lax.broadcasted_iota(jnp.int32, (c, c), 0) >= lax.broadcasted_iota(jnp.int32, (c, c), 1)
    ones_tril = jnp.where(tri, 1.0, 0.0).astype(F32)

    lw = lw_ref[0]
    g = jnp.dot(ones_tril, lw, preferred_element_type=F32, precision=lax.Precision.HIGHEST)
    gc = g[c - 1:c, :]
    eg = jnp.exp(g)
    eng = jnp.exp(-g)
    egc = jnp.exp(gc - g)
    r_all = r_ref[0]
    a_all = a_ref[0]
    v_all = v_ref[0]
    kkr = k_ref[0] * kkw_ref[...]
    k_all = k_ref[0] * (1.0 + (a_all - 1.0) * kaw_ref[...])
    kka = kkr * a_all
    kk_t = kkr * jnp.exp(g - lw)
    r_t = r_all * eg
    k_h = k_all * eng
    b_h = kka * eng
    k_g = k_all * egc
    b_g = kka * egc
    rk = r_all * k_all * rkw_ref[...]
    dec_ref[0, 0] = jnp.exp(gc)

    units = range(npairs)

    def pair(x, u):
        return _big(x[:, u * c2:(u + 1) * c2], lo_half)

    kk_rs = [lax.rsqrt(jnp.maximum(jnp.sum(jnp.square(pair(kkr, u)), axis=1, keepdims=True), KK_EPS))
             for u in units]
    kkt = [pair(kk_t, u) * kk_rs[u] for u in units]
    rt = [pair(r_t, u) for u in units]
    vb = [pair(v_all, u) for u in units]
    bg = [pair(b_g, u) * kk_rs[u] for u in units]
    for u in units:
        bonus_ref[0, :, u * c2:(u + 1) * c2] = _fold(jnp.sum(pair(rk, u), axis=1, keepdims=True) * vb[u])
    gm = [_bdot_nt(jnp.concatenate([kkt[u], rt[u]], axis=0),
                   jnp.concatenate([pair(b_h, u) * kk_rs[u], pair(k_h, u)], axis=0)) for u in units]
    a_b = [jnp.where(strict, gm[u][:c2, :c2], 0.0) for u in units]
    a_k = [jnp.where(strict, gm[u][:c2, c2:], 0.0) for u in units]
    a_rb = [jnp.where(incl, gm[u][c2:, :c2], 0.0) for u in units]
    a_rk = [jnp.where(incl, gm[u][c2:, c2:], 0.0) for u in units]
    x2 = [_bdot(jnp.concatenate([a_k[u], a_rk[u]], axis=0), vb[u]) for u in units]

    d = [jnp.where(same_blk, a_b[u], 0.0) for u in units]
    low = [a_b[u] - d[u] for u in units]
    p = [eye - d[u] for u in units]
    dk = [_bdot(d[u], d[u]) for u in units]
    levels = int(math.log2(SUB)) - 1
    for lvl in range(levels):
        if lvl < levels - 1:
            both = [_bdot(jnp.concatenate([p[u], dk[u]], axis=0), dk[u]) for u in units]
            p = [p[u] + both[u][:c2] for u in units]
            dk = [both[u][c2:] for u in units]
        else:
            p = [p[u] + _bdot(p[u], dk[u]) for u in units]
    e = [_bdot(p[u], low[u]) for u in units]
    q = [eye - e[u] for u in units]
    ek = e
    for _ in range(int(math.log2(CHUNK // SUB)) - 1):
        ek = [_bdot(ek[u], ek[u]) for u in units]
        q = [q[u] + _bdot(q[u], ek[u]) for u in units]
    t_inv = [_bdot(q[u], p[u]) for u in units]

    wu = [_bdot(t_inv[u], jnp.concatenate([kkt[u], x2[u][:c2]], axis=1)) for u in units]
    aw = [_bdot(a_rb[u], wu[u]) for u in units]
    m0 = [_bdot_tn(-bg[u], wu[u][:, :c2]) for u in units]
    nt = [_bdot_tn(jnp.concatenate([vb[u], wu[u][:, c2:]], axis=0),
                   jnp.concatenate([pair(k_g, u), -bg[u]], axis=0)) for u in units]
    for u in units:
        sl = slice(u * c2, (u + 1) * c2)
        p_ref[0, :, sl] = _fold(rt[u] - aw[u][:, :c2]).astype(p_ref.dtype)
        y1_ref[0, :, sl] = _fold(x2[u][c2:] - aw[u][:, c2:])
        m_ref[0, :, sl] = _fold(m0[u]).astype(m_ref.dtype)
        n_ref[0, :, sl] = _fold(nt[u])


def _scan_seq_kernel(p_ref, m_ref, y1_ref, n_ref, dec_ref, bonus_ref, g_ref, lnw_ref, lnb_ref,
                     o_ref, s_ref, *, npairs):
    c2 = 2 * CHUNK

    @pl.when(pl.program_id(2) == 0)
    def _():
        s_ref[...] = jnp.zeros_like(s_ref)

    lo_half = lax.broadcasted_iota(jnp.int32, (CHUNK, c2), 1) < HEAD
    own = ((lax.broadcasted_iota(jnp.int32, (c2, c2), 0) < CHUNK)
           == (lax.broadcasted_iota(jnp.int32, (c2, c2), 1) < HEAD))
    for u in range(npairs):
        sl = slice(u * c2, (u + 1) * c2)
        s = s_ref[u]
        y = _bdot_nt(_big(p_ref[0, :, sl], lo_half), s) + _big(y1_ref[0, :, sl], lo_half)
        s_ref[u] = (s * dec_ref[0, 0, :, sl] + _bdot_nt(s, _big(m_ref[0, :, sl], lo_half))
                    + _big(n_ref[0, :, sl], lo_half))
        mu = jnp.sum(y, axis=1, keepdims=True) * (1.0 / HEAD)
        yc = jnp.where(own, y - mu, 0.0)
        var = jnp.sum(yc * yc, axis=1, keepdims=True) * (1.0 / HEAD)
        yn = _fold(yc * lax.rsqrt(var + GN_EPS)) * lnw_ref[:, sl] + lnb_ref[:, sl]
        o_ref[0, :, sl] = ((yn + bonus_ref[0, :, sl]) * g_ref[0, :, sl]).astype(o_ref.dtype)


def _rwkv_mix(r, lw, k, v, a, g, k_k, k_a, r_k, lnx_w, lnx_b):
    bsz, tp, d = r.shape
    nc = tp // CHUNK
    lanes = _divisor(d, 2048, 128)
    spec = pl.BlockSpec((1, CHUNK, lanes), lambda i, j, c: (i, c, j))
    dspec = pl.BlockSpec((1, 1, 1, lanes), lambda i, j, c: (i, c, 0, j))
    rowspec = pl.BlockSpec((1, lanes), lambda i, j, c: (0, j))
    tile = jax.ShapeDtypeStruct((bsz, tp, d), F32)
    tile16 = jax.ShapeDtypeStruct((bsz, tp, d), BF16)
    p, m0, y1, nt, dec, bonus = pl.pallas_call(
        functools.partial(_scan_prep_kernel, npairs=lanes // 128),
        grid=(bsz, d // lanes, nc),
        in_specs=[spec] * 5 + [rowspec] * 3,
        out_specs=[spec, spec, spec, spec, dspec, spec],
        out_shape=[tile16, tile16, tile, tile, jax.ShapeDtypeStruct((bsz, nc, 1, d), F32), tile],
        compiler_params=_params("parallel", "parallel", "parallel"),
        name="rwkv_scan_prep",
    )(r, lw, k, v, a, k_k.reshape(1, d), k_a.reshape(1, d), r_k.reshape(1, d))
    lanes = _divisor(d, 2048, 128)
    spec = pl.BlockSpec((1, CHUNK, lanes), lambda i, j, c: (i, c, j))
    dspec = pl.BlockSpec((1, 1, 1, lanes), lambda i, j, c: (i, c, 0, j))
    rowspec = pl.BlockSpec((1, lanes), lambda i, j, c: (0, j))
    return pl.pallas_call(
        functools.partial(_scan_seq_kernel, npairs=lanes // 128),
        grid=(bsz, d // lanes, nc),
        in_specs=[spec, spec, spec, spec, dspec, spec, spec, rowspec, rowspec],
        out_specs=spec,
        out_shape=tile16,
        scratch_shapes=[pltpu.VMEM((lanes // 128, 2 * CHUNK, 2 * CHUNK), F32)],
        compiler_params=_params("parallel", "parallel", "arbitrary"),
        name="rwkv_scan_seq",
    )(p, m0, y1, nt, dec, bonus, g, lnx_w.reshape(1, d), lnx_b.reshape(1, d))


HALO = 32


CONV_STRIP = 512


def _conv_kernel(prev_ref, cur_ref, w_ref, b_ref, lnw_ref, lnb_ref, o_ref, buf_ref, acc_ref, sh_ref, *, width, tt, strip):
    first = pl.program_id(1) == 0
    buf_ref[0:HALO, :] = jnp.where(first, 0.0, prev_ref[0])
    buf_ref[HALO:HALO + tt, :] = cur_ref[0]
    off = HALO - (width - 1)
    for s in range(buf_ref.shape[1] // strip):
        lanes = slice(s * strip, (s + 1) * strip)
        acc = jnp.zeros((tt, strip), F32) + b_ref[:, lanes]
        for res in range(SUBLANES):
            span = tt + HALO - (SUBLANES if res else 0)
            if res:
                sh_ref[0:span, :] = buf_ref[res:res + span, lanes]
            for base in range(0, span - tt + 1, SUBLANES):
                j = base + res - off
                if 0 <= j < width:
                    src = sh_ref[base:base + tt, :] if res else buf_ref[base:base + tt, lanes]
                    acc = acc + src * w_ref[j:j + 1, lanes]
        acc_ref[:, lanes] = acc
    z = acc_ref[...]
    mu = jnp.mean(z, axis=-1, keepdims=True)
    zc = z - mu
    var = jnp.mean(zc * zc, axis=-1, keepdims=True)
    zn = zc * lax.rsqrt(var + LN_EPS) * lnw_ref[...] + lnb_ref[...]
    o_ref[0] = (zn * jax.nn.sigmoid(zn)).astype(o_ref.dtype)


def _dwconv_ln_swish(x, w, bias, ln_w, ln_b):
    bsz, tp, d = x.shape
    width = w.shape[0]
    assert width - 1 <= HALO
    tt = _divisor(tp, 320, HALO)
    ratio = tt // HALO
    wpad = jnp.pad(w, ((0, HALO - width), (0, 0)))
    row = pl.BlockSpec((1, d), lambda i, t: (0, 0))
    strip = _divisor(d, CONV_STRIP, 128)
    return pl.pallas_call(
        functools.partial(_conv_kernel, width=width, tt=tt, strip=strip),
        grid=(bsz, tp // tt),
        in_specs=[
            pl.BlockSpec((1, HALO, d), lambda i, t: (i, jnp.maximum(t * ratio - 1, 0), 0)),
            pl.BlockSpec((1, tt, d), lambda i, t: (i, t, 0)),
            pl.BlockSpec((HALO, d), lambda i, t: (0, 0)),
            row, row, row,
        ],
        out_specs=pl.BlockSpec((1, tt, d), lambda i, t: (i, t, 0)),
        out_shape=jax.ShapeDtypeStruct((bsz, tp, d), BF16),
        scratch_shapes=[pltpu.VMEM((HALO + tt, d), F32), pltpu.VMEM((tt, d), F32),
                        pltpu.VMEM((HALO + tt, strip), F32)],
        compiler_params=_params("parallel", "parallel"),
        name="dwconv_ln_swish",
    )(x, x, wpad, bias.reshape(1, d), ln_w.reshape(1, d), ln_b.reshape(1, d))


NEG = -jnp.inf
N_CAND = 16 + 7 * 8 + 8


def _cand_static():
    row = lax.broadcasted_iota(jnp.int32, (N_CAND, 1), 0)
    grp = jnp.where(row < 16, 0, (row - 16) // 8 + 1)
    bmid = (row - 16) % 8
    last = row >= 72
    a = jnp.where(row < 16, 0, jnp.where(last, row - 64, grp))
    b = jnp.where(row < 16, row, jnp.where(last, 0, bmid))
    valid = (a + 1) * (b + 1) <= PEER_TOPK
    return a * PEER_TOPK + b, valid


def _retrieve_kernel(q_ref, keys_ref, g_ref, i_ref, j_ref,
                     sv_ref, si_ref, cs_ref, cc_ref, ts_ref, tc_ref, kiota_ref, *, nheads, nkeys, tt):
    kiota_ref[...] = lax.broadcasted_iota(jnp.int32, (nkeys, tt), 0).astype(F32)
    flat, valid = _cand_static()
    code0 = (flat * (nkeys * nkeys)).astype(F32)
    big = float(1 << 24)

    def sub_keys(h):
        slot = h % 2
        s = []
        for half in range(2):
            hc = h * 2 + half
            q = q_ref[hc * nkeys:(hc + 1) * nkeys, :]
            s.append(jnp.dot(keys_ref[hc], q.astype(BF16), preferred_element_type=F32))
        for r in range(PEER_TOPK):
            for half in range(2):
                m = jnp.max(s[half], axis=0, keepdims=True)
                idx = jnp.min(jnp.where(s[half] == m, kiota_ref[...], big), axis=0, keepdims=True)
                sv_ref[slot, half, r:r + 1, :] = m
                si_ref[slot, half, r:r + 1, :] = idx
                s[half] = jnp.where(kiota_ref[...] == idx, NEG, s[half])

    def experts(h):
        sv = sv_ref.at[h % 2]
        si = si_ref.at[h % 2]
        cs_ref[0:16, :] = sv[0, 0:1, :] + sv[1, :, :]
        cc_ref[0:16, :] = si[0, 0:1, :] * nkeys + si[1, :, :]
        for a in range(1, 8):
            lo = 16 + (a - 1) * 8
            cs_ref[lo:lo + 8, :] = sv[0, a:a + 1, :] + sv[1, 0:8, :]
            cc_ref[lo:lo + 8, :] = si[0, a:a + 1, :] * nkeys + si[1, 0:8, :]
        cs_ref[72:80, :] = sv[0, 8:16, :] + sv[1, 0:1, :]
        cc_ref[72:80, :] = si[0, 8:16, :] * nkeys + si[1, 0:1, :]
        cand = jnp.where(valid, cs_ref[...], NEG)
        cc_ref[...] = cc_ref[...] + code0
        for r in range(PEER_TOPK):
            m = jnp.max(cand, axis=0, keepdims=True)
            c = jnp.min(jnp.where(cand == m, cc_ref[...], big), axis=0, keepdims=True)
            ts_ref[r:r + 1, :] = m
            tc_ref[r:r + 1, :] = c
            cand = jnp.where(cc_ref[...] == c, NEG, cand)
        rows = slice(h * PEER_TOPK, (h + 1) * PEER_TOPK)
        top = ts_ref[...]
        ex = jnp.exp(top - jnp.max(top, axis=0, keepdims=True))
        g_ref[rows, :] = ex / jnp.sum(ex, axis=0, keepdims=True)
        expert = tc_ref[...].astype(jnp.int32) % (nkeys * nkeys)
        i_ref[rows, :] = expert // nkeys
        j_ref[rows, :] = expert % nkeys

    sub_keys(0)
    for h in range(nheads):
        if h + 1 < nheads:
            sub_keys(h + 1)
        experts(h)


def _peer_retrieve(qt, keys):
    ph, _, nkeys, dk = keys.shape
    m = qt.shape[1]
    assert nkeys == dk == 128
    tt = 128
    keys2 = keys.reshape(ph * 2, nkeys, dk).astype(BF16)
    np_ = ph * PEER_TOPK
    out_spec = pl.BlockSpec((np_, tt), lambda i: (0, i))
    return pl.pallas_call(
        functools.partial(_retrieve_kernel, nheads=ph, nkeys=nkeys, tt=tt),
        grid=(m // tt,),
        in_specs=[
            pl.BlockSpec((ph * 2 * dk, tt), lambda i: (0, i)),
            pl.BlockSpec((ph * 2, nkeys, dk), lambda i: (0, 0, 0)),
        ],
        out_specs=[out_spec, out_spec, out_spec],
        out_shape=[
            jax.ShapeDtypeStruct((np_, m), F32),
            jax.ShapeDtypeStruct((np_, m), jnp.int32),
            jax.ShapeDtypeStruct((np_, m), jnp.int32),
        ],
        scratch_shapes=[
            pltpu.VMEM((2, 2, PEER_TOPK, tt), F32),
            pltpu.VMEM((2, 2, PEER_TOPK, tt), F32),
            pltpu.VMEM((N_CAND, tt), F32),
            pltpu.VMEM((N_CAND, tt), F32),
            pltpu.VMEM((PEER_TOPK, tt), F32),
            pltpu.VMEM((PEER_TOPK, tt), F32),
            pltpu.VMEM((nkeys, tt), F32),
        ],
        compiler_params=_params("parallel"),
        name="peer_retrieve",
    )(qt, keys2)


DENSE_SUB = 512


def _gate_table_kernel(g_ref, i_ref, j_ref, o_ref, *, tb, nkeys):
    sub = lax.broadcasted_iota(jnp.int32, (nkeys, g_ref.shape[1]), 0)

    def body(t, carry):
        g = g_ref[pl.ds(t, 1), :]
        i = i_ref[pl.ds(t, 1), :]
        j = j_ref[pl.ds(t, 1), :]
        a_t = jnp.where(i == sub, g, 0.0).astype(BF16)
        b_t = jnp.where(j == sub, 1.0, 0.0).astype(BF16)
        w = lax.dot_general(a_t, b_t, (((1,), (1,)), ((), ())), preferred_element_type=F32)
        o_ref[:, pl.ds(t, 1)] = w.reshape(nkeys // SUBLANES, 1, SUBLANES, nkeys)
        return carry

    lax.fori_loop(0, tb, body, 0, unroll=64)


def _gate_table(g, i, j, nkeys):
    m, p = g.shape
    tb = _divisor(m, 64, 8)
    spec = pl.BlockSpec((tb, p), lambda t: (t, 0))
    ng = nkeys // SUBLANES
    w = pl.pallas_call(
        functools.partial(_gate_table_kernel, tb=tb, nkeys=nkeys),
        grid=(m // tb,),
        in_specs=[spec, spec, spec],
        out_specs=pl.BlockSpec((ng, tb, SUBLANES, nkeys), lambda t: (0, t, 0, 0)),
        out_shape=jax.ShapeDtypeStruct((ng, m, SUBLANES, nkeys), F32),
        compiler_params=_params("parallel"),
        name="peer_gate_table",
    )(g, i, j)
    return w.reshape(ng, m * SUBLANES, nkeys)


def _peer_dense_kernel(*refs, tm, nkeys, final_norm):
    if final_norm:
        h_ref, g_ref, u_ref, v_ref, w_ref, gf_ref, o_ref, x_ref = refs
    else:
        h_ref, g_ref, u_ref, v_ref, w_ref, o_ref, x_ref = refs
    k = pl.program_id(1)

    @pl.when(k == 0)
    def _():
        h = h_ref[...]
        o_ref[...] = h
        x_ref[...] = _rms(h, g_ref[...]).astype(BF16)

    per_sub = DENSE_SUB // nkeys
    for s in range(SUBLANES // per_sub):
        cols = slice(s * DENSE_SUB, (s + 1) * DENSE_SUB)
        act = lax.dot_general(x_ref[...], u_ref[cols, :], (((1,), (1,)), ((), ())),
                              preferred_element_type=F32)
        gelu = 0.5 * act * (1.0 + lax.erf(act * (1.0 / math.sqrt(2.0))))
        w = jnp.concatenate(
            [w_ref[pl.ds(s * per_sub + i, tm, stride=SUBLANES), :] for i in range(per_sub)], axis=1)
        o_ref[...] += jnp.dot((w * gelu).astype(BF16), v_ref[cols, :], preferred_element_type=F32)

    if final_norm:
        @pl.when(k == pl.num_programs(1) - 1)
        def _():
            o_ref[...] = _rms(o_ref[...], gf_ref[...])


def _peer_dense(h, norm_g, u, v, layer, w, final_g=None):
    m, d = h.shape
    ng, _, nkeys = w.shape
    assert DENSE_SUB % nkeys == 0 and SUBLANES * nkeys % DENSE_SUB == 0
    te = SUBLANES * nkeys
    tm = _divisor(m, 832, 64)
    row = pl.BlockSpec((1, d), lambda i, k: (0, 0))
    args = [h, norm_g.reshape(1, d), u, v, w]
    specs = [
        pl.BlockSpec((tm, d), lambda i, k: (i, 0), pipeline_mode=pl.Buffered(1)),
        row,
        pl.BlockSpec((None, te, d), lambda i, k: (layer, k, 0)),
        pl.BlockSpec((None, te, d), lambda i, k: (layer, k, 0)),
        pl.BlockSpec((None, tm * SUBLANES, nkeys), lambda i, k: (k, i, 0)),
    ]
    if final_g is not None:
        args.append(final_g.reshape(1, d))
        specs.append(row)
    return pl.pallas_call(
        functools.partial(_peer_dense_kernel, tm=tm, nkeys=nkeys, final_norm=final_g is not None),
        grid=(m // tm, ng),
        in_specs=specs,
        out_specs=pl.BlockSpec((tm, d), lambda i, k: (i, 0)),
        out_shape=jax.ShapeDtypeStruct((m, d), F32),
        scratch_shapes=[pltpu.VMEM((tm, d), BF16)],
        compiler_params=_params("parallel", "arbitrary"),
        name="peer_dense",
    )(*args)


def _peer_layer(h, g_norm, wq, keys, u_tabs, v_tabs, layer, final_g=None):
    nkeys = keys.shape[2]
    qt = _proj_t(wq.T.astype(BF16), h, g_norm)
    gate, ii, jj = _peer_retrieve(qt, keys)
    w = _gate_table(gate.T, ii.T, jj.T, nkeys)
    return _peer_dense(h, g_norm, u_tabs, v_tabs, layer, w, final_g)


def _rwkv_layer(h, tp, g_norm, mix, w0, w1, w2, a0, a1, a2, g1, g2, k_k, k_a, r_k, w_rkv, w_o, lnx_w, lnx_b):
    m, d = h.shape
    bsz = m // tp
    nh = d // HEAD

    def shifted(n, weights, **kw):
        return _proj(h, weights, norm_g=g_norm, mix=mix[n], seq_len=tp, **kw)

    r = shifted(0, (w_rkv[0], None))
    k = shifted(2, (w_rkv[1], None))
    v = shifted(3, (w_rkv[2], None))
    lw, a, g = _lora3(h, g_norm, jnp.stack([mix[1], mix[4], mix[5]]), tp,
                      [(w1, w2, w0), (a1, a2, a0), (g1, g2, None)],
                      mids=("tanh", None, "sigmoid"), posts=("log_decay", "sigmoid", None))

    def seq(z):
        return z.reshape(bsz, tp, d)

    out = _rwkv_mix(seq(r), seq(lw), seq(k), seq(v), seq(a), seq(g), k_k, k_a, r_k, lnx_w, lnx_b)
    return _proj(out.reshape(m, d), (w_o, None), residual=h)


def _conv_layer(h, tp, g_norm, pw1_w, pw1_b, dw_w, dw_b, ln_w, ln_b, pw2_w, pw2_b):
    m, d = h.shape
    z = _proj(h, (pw1_w, pw1_b), body="glu", norm_g=g_norm)
    z = _dwconv_ln_swish(z.reshape(m // tp, tp, d), dw_w, dw_b, ln_w, ln_b).reshape(m, d)
    return _proj(z, (pw2_w, pw2_b), residual=h)


def kernel(x, meta_tokens, norm_mix_a, rwkv_mix, rwkv_w0, rwkv_w1, rwkv_w2, rwkv_a0, rwkv_a1, rwkv_a2, rwkv_g1, rwkv_g2, rwkv_k_k, rwkv_k_a, rwkv_r_k, rwkv_w_rkv, rwkv_w_o, rwkv_lnx_w, rwkv_lnx_b, norm_mix_b, conv_pw1_w, conv_pw1_b, conv_dw_w, conv_dw_b, conv_ln_w, conv_ln_b, conv_pw2_w, conv_pw2_b, norm_ffn, peer_wq, peer_keys, peer_u, peer_v, norm_final):
    bsz, seq, d = x.shape
    n_meta = meta_tokens.shape[0]
    t = n_meta + seq
    tp = -(-t // 128) * 128 if t <= 128 else -(-t // CHUNK) * CHUNK
    depth = norm_ffn.shape[0]
    meta = jnp.broadcast_to(meta_tokens[None].astype(x.dtype), (bsz, n_meta, d))
    h = jnp.concatenate([meta, x, jnp.zeros((bsz, tp - t, d), x.dtype)], axis=1).reshape(bsz * tp, d)
    u_tabs = peer_u.astype(BF16)
    v_tabs = peer_v.astype(BF16)
    for i in range(depth):
        j = i // 2
        if i % 2 == 0:
            h = _rwkv_layer(
                h, tp, norm_mix_a[j], rwkv_mix[j], rwkv_w0[j], rwkv_w1[j], rwkv_w2[j], rwkv_a0[j], rwkv_a1[j],
                rwkv_a2[j], rwkv_g1[j], rwkv_g2[j], rwkv_k_k[j], rwkv_k_a[j], rwkv_r_k[j], rwkv_w_rkv[j],
                rwkv_w_o[j], rwkv_lnx_w[j], rwkv_lnx_b[j])
        else:
            h = _conv_layer(
                h, tp, norm_mix_b[j], conv_pw1_w[j], conv_pw1_b[j], conv_dw_w[j], conv_dw_b[j], conv_ln_w[j],
                conv_ln_b[j], conv_pw2_w[j], conv_pw2_b[j])
        h = _peer_layer(h, norm_ffn[i], peer_wq[i], peer_keys[i], u_tabs, v_tabs, i,
                        norm_final if i == depth - 1 else None)
    if depth == 0:
        h = _rms(h, norm_final)
    return h.reshape(bsz, tp, d)[:, n_meta:t]
```
